```python
import jax, jax.numpy as jnp
from jax import lax
import numpy as np

D_MODEL = 1024
BATCH = 4
SEQ = 4096
DEPTH = 4
DEC_BATCH = 128
DEC_SEQ = 4
PAST_LEN = 2048
PAGE_SIZE = 128

N_A = DEPTH // 2
N_B = DEPTH - N_A
RWKV_HEAD_DIM = 64
RWKV_HEADS = D_MODEL // RWKV_HEAD_DIM
DECAY_LORA = 64
AAA_LORA = 64
MV_LORA = 32
GATE_LORA = 160
NSA_HEADS = 16
NSA_KV_HEADS = 4
NSA_HEAD_DIM = D_MODEL // NSA_HEADS
NSA_GROUP = NSA_HEADS // NSA_KV_HEADS
CMP_STRIDE = 16
CMP_LEN = 32
CMP_R = CMP_LEN // CMP_STRIDE
CMP_HIDDEN = 256
SEL_BLOCK = 64
SEL_TOP = 16
WINDOW = 512
Q_BLOCK = 64
N_GROUPS = 4
GROUP_EXPERTS = 4
N_EXPERTS = N_GROUPS * GROUP_EXPERTS
EXPERT_TOP = 2
EXPERT_FF = 512
NORM_EPS = 1e-6
LNX_EPS = 64e-5
NEG = -1e30

kernel_name = 'yoco_rwkv7_nsa_hmoe_step'


def rms_norm(x, g):
    xf = x.astype(jnp.float32)
    y = xf * lax.rsqrt(jnp.mean(xf * xf, axis=-1, keepdims=True) + NORM_EPS)
    return (y * g.astype(jnp.float32)).astype(x.dtype)


def modulate(h, shift, scale):
    return h * (1 + scale[:, None, :]) + shift[:, None, :]


def masked_softmax(s, mask):
    return jax.nn.softmax(jnp.where(mask, s, NEG), axis=-1) * mask


def alibi_slopes():
    h = jnp.arange(1, NSA_HEADS + 1, dtype=jnp.float32)
    return jnp.exp2(-8.0 * h / NSA_HEADS).reshape(NSA_KV_HEADS, NSA_GROUP)


def wkv7_scan(r, w, k, v, a, b, s0):
    def step(s, inp):
        r_t, w_t, k_t, v_t, a_t, b_t = inp
        sa = jnp.einsum('bhij,bhj->bhi', s, a_t)
        s = s * w_t[:, :, None, :] + sa[..., None] * b_t[:, :, None, :] + v_t[..., None] * k_t[:, :, None, :]
        return s, jnp.einsum('bhij,bhj->bhi', s, r_t)
    xs = tuple(jnp.moveaxis(z.astype(jnp.float32), 1, 0) for z in (r, w, k, v, a, b))
    s, ys = lax.scan(step, s0.astype(jnp.float32), xs)
    return jnp.moveaxis(ys, 0, 1), s


def rwkv7_time_mix(h, shift_prev, s0, v_first, P, l):
    B, T, C = h.shape
    heads = lambda z: z.reshape(B, T, RWKV_HEADS, RWKV_HEAD_DIM)
    xx = jnp.concatenate([shift_prev.astype(h.dtype)[:, None, :], h[:, :-1]], axis=1) - h
    xm = h[None] + xx[None] * P['rwkv_mix'][l][:, None, None, :]
    rkv = jnp.einsum('sbtc,scd->sbtd', xm[:3], P['rwkv_w_rkv'][l])
    r, k, v = rkv[0], rkv[1], rkv[2]
    w = -jax.nn.softplus(-(P['rwkv_w0'][l] + jnp.tanh(xm[3] @ P['rwkv_w1'][l]) @ P['rwkv_w2'][l])) - 0.5
    if v_first is None:
        v_first = v
    else:
        v = v + (v_first - v) * jax.nn.sigmoid(P['rwkv_v0'][l - 1] + (xm[2] @ P['rwkv_v1'][l - 1]) @ P['rwkv_v2'][l - 1])
    a = jax.nn.sigmoid(P['rwkv_a0'][l] + (xm[4] @ P['rwkv_a1'][l]) @ P['rwkv_a2'][l])
    g = jax.nn.sigmoid(xm[5] @ P['rwkv_g1'][l]) @ P['rwkv_g2'][l]
    kk = heads(k * P['rwkv_k_k'][l]).astype(jnp.float32)
    kk = kk * lax.rsqrt(jnp.maximum(jnp.sum(kk * kk, axis=-1, keepdims=True), 1e-24))
    k = k * (1 + (a - 1) * P['rwkv_k_a'][l])
    a_h = heads(a).astype(jnp.float32)
    decay = jnp.exp(-jnp.exp(heads(w).astype(jnp.float32)))
    y, s = wkv7_scan(heads(r), decay, heads(k), heads(v), -kk, kk * a_h, s0)
    mu = jnp.mean(y, axis=-1, keepdims=True)
    var = jnp.mean(jnp.square(y - mu), axis=-1, keepdims=True)
    yn = ((y - mu) * lax.rsqrt(var + LNX_EPS)).reshape(B, T, C) * P['rwkv_lnx_w'][l] + P['rwkv_lnx_b'][l]
    bonus = jnp.sum(heads(r).astype(jnp.float32) * heads(k).astype(jnp.float32) * P['rwkv_r_k'][l], axis=-1, keepdims=True) * heads(v).astype(jnp.float32)
    out = ((yn + bonus.reshape(B, T, C)).astype(h.dtype) * g) @ P['rwkv_w_o'][l]
    return out, v_first, h[:, -1], s.astype(s0.dtype)


def hier_moe(h, P, l):
    B, T, C = h.shape
    hf = h.reshape(B * T, C)
    grp_logits = (hf @ P['moe_w_grp'][l] + P['moe_b_grp'][l]).astype(jnp.float32)
    grp = jnp.argmax(grp_logits, axis=-1)
    p_grp = jnp.take_along_axis(jax.nn.softmax(grp_logits, axis=-1), grp[:, None], axis=-1)
    exp_logits = (hf @ P['moe_w_exp'][l] + P['moe_b_exp'][l]).astype(jnp.float32).reshape(-1, N_GROUPS, GROUP_EXPERTS)
    in_grp = jnp.take_along_axis(exp_logits, grp[:, None, None], axis=1)[:, 0]
    top_val, top_idx = lax.top_k(in_grp, EXPERT_TOP)
    w_top = jax.nn.softmax(top_val, axis=-1) * p_grp
    eid = grp[:, None] * GROUP_EXPERTS + top_idx
    combine = jnp.einsum('nk,nke->ne', w_top, jax.nn.one_hot(eid, N_EXPERTS, dtype=jnp.float32)).astype(h.dtype)
    out = jnp.zeros_like(hf)
    for e in range(N_EXPERTS):
        act = jax.nn.silu(hf @ P['moe_w_gate'][l, e]) * (hf @ P['moe_w_up'][l, e])
        out = out + combine[:, e:e + 1] * (act @ P['moe_w_down'][l, e])
    return out.reshape(B, T, C)


def shared_kv(x, c_act, P):
    B, T, _ = x.shape
    sh, sc = jnp.split(c_act @ P['kv_ada_w'] + P['kv_ada_b'], 2, axis=-1)
    hk = modulate(rms_norm(x, P['kv_norm_g']), sh, sc)
    return (hk @ P['w_kv']).reshape(B, T, 6, NSA_KV_HEADS, NSA_HEAD_DIM)


def compress(kc, vc, cmp_pos, cmp_w1, cmp_b1, cmp_w2):
    B, T = kc.shape[:2]
    n_ch = T // CMP_STRIDE
    nc = n_ch - CMP_R + 1
    kv = jnp.stack([kc, vc])[:, :, :n_ch * CMP_STRIDE]
    kv = kv.reshape(2, B, n_ch, CMP_STRIDE, NSA_KV_HEADS, NSA_HEAD_DIM)
    w1 = cmp_w1.reshape(2, CMP_R, CMP_STRIDE, NSA_HEAD_DIM, CMP_HIDDEN)
    part = jnp.einsum('zbnsgd,zrsdh->zrbngh', kv, w1)
    pos_term = jnp.einsum('zrsd,zrsdh->zh', cmp_pos.reshape(2, CMP_R, CMP_STRIDE, NSA_HEAD_DIM), w1) + cmp_b1
    hid = pos_term[:, None, None, None, :]
    for r in range(CMP_R):
        hid = hid + part[:, r, :, r:r + nc]
    out = jnp.einsum('zbngh,zhd->zbngd', jax.nn.gelu(hid), cmp_w2)
    return out[0], out[1]


def selection_blocks(ks, vs):
    B, T = ks.shape[:2]
    nb = -(-T // SEL_BLOCK)
    kv = jnp.pad(jnp.stack([ks, vs]), ((0, 0), (0, 0), (0, nb * SEL_BLOCK - T), (0, 0), (0, 0)))
    kv = kv.reshape(2, B, nb, SEL_BLOCK, NSA_KV_HEADS, NSA_HEAD_DIM).transpose(0, 1, 4, 2, 3, 5)
    return kv[0], kv[1]


def overlap_matrix(nc, nb):
    st = jnp.arange(nc, dtype=jnp.int32) * CMP_STRIDE
    bs = jnp.arange(nb, dtype=jnp.int32) * SEL_BLOCK
    ov = (st[:, None] <= bs[None, :] + SEL_BLOCK - 1) & (st[:, None] + CMP_LEN - 1 >= bs[None, :])
    return ov.astype(jnp.float32)


def branch_context(kv4, cmp_pos, cmp_w1, cmp_b1, cmp_w2):
    kc, vc = compress(kv4[:, :, 0], kv4[:, :, 1], cmp_pos, cmp_w1, cmp_b1, cmp_w2)
    nc = kc.shape[1]
    cmp_end = jnp.arange(nc, dtype=jnp.int32) * CMP_STRIDE + (CMP_LEN - 1)
    ksb, vsb = selection_blocks(kv4[:, :, 2], kv4[:, :, 3])
    return kc, vc, cmp_end, ksb, vsb, overlap_matrix(nc, ksb.shape[2])


def nsa_attend(q, gates, q_pos, kc, vc, cmp_end, ksb, vsb, overlap, kw, vw, w_pos):
    B, Tq = q.shape[:2]
    G, Hg, Dh = NSA_KV_HEADS, NSA_GROUP, NSA_HEAD_DIM
    qg = q.reshape(B, Tq, G, Hg, Dh).astype(jnp.float32) * (Dh ** -0.5)
    slope = alibi_slopes()[None, None, :, :, None]
    d_c = (q_pos[:, None] - cmp_end[None, :])[None, :, None, None, :]
    s_c = jnp.einsum('bqgpd,bngd->bqgpn', qg, kc.astype(jnp.float32)) - slope * d_c
    p_c = masked_softmax(s_c, d_c >= 0)
    o_c = jnp.einsum('bqgpn,bngd->bqgpd', p_c, vc.astype(jnp.float32))
    nb = ksb.shape[2]
    blk = jnp.arange(nb, dtype=jnp.int32)[None, :]
    cur = (q_pos // SEL_BLOCK)[:, None]
    valid = blk * SEL_BLOCK <= q_pos[:, None]
    forced = (blk == 0) | (blk == cur) | (blk == cur - 1)
    imp = jnp.einsum('bqgpn,nj->bqgj', p_c, overlap)
    score = jnp.where(valid[None, :, None, :], jnp.where(forced[None, :, None, :], -NEG, imp), NEG)
    n_top = min(SEL_TOP, nb)
    _, idx = lax.top_k(score, n_top)
    bi = jnp.arange(B)[:, None, None, None]
    gi = jnp.arange(G)[None, None, :, None]
    k_s = ksb[bi, gi, idx].reshape(B, Tq, G, n_top * SEL_BLOCK, Dh).astype(jnp.float32)
    v_s = vsb[bi, gi, idx].reshape(B, Tq, G, n_top * SEL_BLOCK, Dh).astype(jnp.float32)
    pos = (idx[..., None] * SEL_BLOCK + jnp.arange(SEL_BLOCK, dtype=jnp.int32)).reshape(B, Tq, G, n_top * SEL_BLOCK)
    d_s = (q_pos[None, :, None, None] - pos)[:, :, :, None, :]
    s_s = jnp.einsum('bqgpd,bqgkd->bqgpk', qg, k_s) - slope * d_s
    p_s = masked_softmax(s_s, d_s >= 0)
    o_s = jnp.einsum('bqgpk,bqgkd->bqgpd', p_s, v_s)
    d_w = q_pos[:, None] - w_pos[None, :]
    m_w = ((d_w >= 0) & (d_w < WINDOW) & (w_pos[None, :] >= 0))[None, :, None, None, :]
    s_w = jnp.einsum('bqgpd,bwgd->bqgpw', qg, kw.astype(jnp.float32)) - slope * d_w[None, :, None, None, :]
    p_w = masked_softmax(s_w, m_w)
    o_w = jnp.einsum('bqgpw,bwgd->bqgpd', p_w, vw.astype(jnp.float32))
    g = gates.reshape(B, Tq, G, Hg, 3).astype(jnp.float32)
    o = g[..., 0:1] * o_c + g[..., 1:2] * o_s + g[..., 2:3] * o_w
    return o.reshape(B, Tq, G * Hg * Dh).astype(q.dtype)


def prompt_attend(q, gates, ctx, kvw_pad):
    B, T = q.shape[:2]
    n_qb = T // Q_BLOCK
    span = WINDOW + Q_BLOCK
    qb = jnp.swapaxes(q.reshape(B, n_qb, Q_BLOCK, NSA_HEADS, NSA_HEAD_DIM), 0, 1)
    gb = jnp.swapaxes(gates.reshape(B, n_qb, Q_BLOCK, NSA_HEADS, 3), 0, 1)

    def one_block(args):
        i, q_i, g_i = args
        start = i * Q_BLOCK
        kvw = lax.dynamic_slice_in_dim(kvw_pad, start, span, axis=1)
        q_pos = start + jnp.arange(Q_BLOCK, dtype=jnp.int32)
        w_pos = start - WINDOW + jnp.arange(span, dtype=jnp.int32)
        return nsa_attend(q_i, g_i, q_pos, *ctx, kvw[:, :, 0], kvw[:, :, 1], w_pos)

    out = lax.map(one_block, (jnp.arange(n_qb, dtype=jnp.int32), qb, gb))
    return jnp.swapaxes(out, 0, 1).reshape(B, T, -1)


def nsa_mix(h, w_qg, w_o, attend):
    B, T, _ = h.shape
    qg = h @ w_qg
    q = qg[..., :NSA_HEADS * NSA_HEAD_DIM].reshape(B, T, NSA_HEADS, NSA_HEAD_DIM)
    gates = jax.nn.sigmoid(qg[..., NSA_HEADS * NSA_HEAD_DIM:].astype(jnp.float32)).reshape(B, T, NSA_HEADS, 3)
    return attend(q, gates) @ w_o


def run_trunk(x, c, shift0, wkv0, make_attend, P):
    c_act = jax.nn.silu(c)
    v_first, attend, kv = None, None, None
    shifts, states = [], []
    for l in range(DEPTH):
        sh1, sc1, gt1, sh2, sc2, gt2 = jnp.split(c_act @ P['ada_w'][l] + P['ada_b'][l], 6, axis=-1)
        h = modulate(rms_norm(x, P['norm_mix_g'][l]), sh1, sc1)
        if l < N_A:
            out, v_first, last_h, s = rwkv7_time_mix(h, shift0[l], wkv0[l], v_first, P, l)
            shifts.append(last_h)
            states.append(s)
        else:
            out = nsa_mix(h, P['nsa_w_qg'][l - N_A], P['nsa_w_o'][l - N_A], attend)
        x = x + gt1[:, None, :] * out
        h = modulate(rms_norm(x, P['norm_ffn_g'][l]), sh2, sc2)
        x = x + gt2[:, None, :] * hier_moe(h, P, l)
        if l == N_A - 1:
            kv = shared_kv(x, c_act, P)
            attend = make_attend(kv)
    return rms_norm(x, P['norm_f_g']), kv, jnp.stack(shifts), jnp.stack(states)


def setup_inputs(seed: int = 0) -> dict:
    key = jax.random.key(seed)
    keys = iter(jax.random.split(key, 64))

    def nrm(shape, scale, offset=0.0):
        return offset + scale * jax.random.normal(next(keys), shape, jnp.float32)

    D = D_MODEL
    G, Dh = NSA_KV_HEADS, NSA_HEAD_DIM
    H, N = RWKV_HEADS, RWKV_HEAD_DIM
    n_pages = PAST_LEN // PAGE_SIZE
    n_used = DEC_BATCH * n_pages
    n_pool = n_used + max(1, n_used // 4)
    page_table = jax.random.permutation(next(keys), n_pool)[:n_used].reshape(DEC_BATCH, n_pages).astype(jnp.int32)
    wb = min(WINDOW, PAST_LEN)
    return {
        'x_prompt': nrm((BATCH, SEQ, D), 1.0),
        'x_sample': nrm((DEC_BATCH, DEC_SEQ, D), 1.0),
        'cache_nsa_kv': nrm((n_pool, PAGE_SIZE, 4, G, Dh), 1.0),
        'cache_win_kv': nrm((DEC_BATCH, wb, 2, G, Dh), 1.0),
        'state_wkv': nrm((N_A, DEC_BATCH, H, N, N), 0.3),
        'state_shift': nrm((N_A, DEC_BATCH, D), 1.0),
        'page_table': page_table,
        'c_prompt': nrm((BATCH, D), 1.0),
        'c_sample': nrm((DEC_BATCH, D), 1.0),
        'ada_w': nrm((DEPTH, D, 6 * D), 0.5 * D ** -0.5),
        'ada_b': nrm((DEPTH, 6 * D), 0.02),
        'norm_mix_g': nrm((DEPTH, D), 0.02, 1.0),
        'norm_ffn_g': nrm((DEPTH, D), 0.02, 1.0),
        'norm_f_g': nrm((D,), 0.02, 1.0),
        'rwkv_mix': jax.random.uniform(next(keys), (N_A, 6, D), jnp.float32),
        'rwkv_w_rkv': nrm((N_A, 3, D, D), D ** -0.5),
        'rwkv_w0': nrm((N_A, D), 0.5, -1.0),
        'rwkv_w1': nrm((N_A, D, DECAY_LORA), D ** -0.5),
        'rwkv_w2': nrm((N_A, DECAY_LORA, D), 0.1 * DECAY_LORA ** -0.5),
        'rwkv_a0': nrm((N_A, D), 0.1),
        'rwkv_a1': nrm((N_A, D, AAA_LORA), D ** -0.5),
        'rwkv_a2': nrm((N_A, AAA_LORA, D), 0.1 * AAA_LORA ** -0.5),
        'rwkv_v0': nrm((N_A - 1, D), 0.5),
        'rwkv_v1': nrm((N_A - 1, D, MV_LORA), D ** -0.5),
        'rwkv_v2': nrm((N_A - 1, MV_LORA, D), 0.5 * MV_LORA ** -0.5),
        'rwkv_g1': nrm((N_A, D, GATE_LORA), D ** -0.5),
        'rwkv_g2': nrm((N_A, GATE_LORA, D), GATE_LORA ** -0.5),
        'rwkv_k_k': nrm((N_A, D), 0.05, 0.85),
        'rwkv_k_a': nrm((N_A, D), 0.05, 1.0),
        'rwkv_r_k': nrm((N_A, H, N), 0.1),
        'rwkv_lnx_w': nrm((N_A, D), 0.02, 1.0),
        'rwkv_lnx_b': nrm((N_A, D), 0.02),
        'rwkv_w_o': nrm((N_A, D, D), D ** -0.5),
        'kv_norm_g': nrm((D,), 0.02, 1.0),
        'kv_ada_w': nrm((D, 2 * D), 0.5 * D ** -0.5),
        'kv_ada_b': nrm((2 * D,), 0.02),
        'w_kv': nrm((D, 6 * G * Dh), D ** -0.5),
        'cmp_pos': nrm((2, CMP_LEN, Dh), 0.1),
        'cmp_w1': nrm((2, CMP_LEN * Dh, CMP_HIDDEN), (CMP_LEN * Dh) ** -0.5),
        'cmp_b1': nrm((2, CMP_HIDDEN), 0.02),
        'cmp_w2': nrm((2, CMP_HIDDEN, Dh), CMP_HIDDEN ** -0.5),
        'nsa_w_qg': nrm((N_B, D, NSA_HEADS * Dh + 3 * NSA_HEADS), D ** -0.5),
        'nsa_w_o': nrm((N_B, NSA_HEADS * Dh, D), (NSA_HEADS * Dh) ** -0.5),
        'moe_w_grp': nrm((DEPTH, D, N_GROUPS), D ** -0.5),
        'moe_b_grp': nrm((DEPTH, N_GROUPS), 0.01),
        'moe_w_exp': nrm((DEPTH, D, N_EXPERTS), D ** -0.5),
        'moe_b_exp': nrm((DEPTH, N_EXPERTS), 0.01),
        'moe_w_gate': nrm((DEPTH, N_EXPERTS, D, EXPERT_FF), D ** -0.5),
        'moe_w_up': nrm((DEPTH, N_EXPERTS, D, EXPERT_FF), D ** -0.5),
        'moe_w_down': nrm((DEPTH, N_EXPERTS, EXPERT_FF, D), EXPERT_FF ** -0.5),
    }


def reference(x_prompt, x_sample, cache_nsa_kv, cache_win_kv, state_wkv, state_shift, page_table, c_prompt, c_sample,
              ada_w, ada_b, norm_mix_g, norm_ffn_g, norm_f_g,
              rwkv_mix, rwkv_w_rkv, rwkv_w0, rwkv_w1, rwkv_w2, rwkv_a0, rwkv_a1, rwkv_a2, rwkv_v0, rwkv_v1, rwkv_v2,
              rwkv_g1, rwkv_g2, rwkv_k_k, rwkv_k_a, rwkv_r_k, rwkv_lnx_w, rwkv_lnx_b, rwkv_w_o,
              kv_norm_g, kv_ada_w, kv_ada_b, w_kv, cmp_pos, cmp_w1, cmp_b1, cmp_w2,
              nsa_w_qg, nsa_w_o,
              moe_w_grp, moe_b_grp, moe_w_exp, moe_b_exp, moe_w_gate, moe_w_up, moe_w_down):
    P = dict(ada_w=ada_w, ada_b=ada_b, norm_mix_g=norm_mix_g, norm_ffn_g=norm_ffn_g, norm_f_g=norm_f_g,
             rwkv_mix=rwkv_mix, rwkv_w_rkv=rwkv_w_rkv, rwkv_w0=rwkv_w0, rwkv_w1=rwkv_w1, rwkv_w2=rwkv_w2,
             rwkv_a0=rwkv_a0, rwkv_a1=rwkv_a1, rwkv_a2=rwkv_a2, rwkv_v0=rwkv_v0, rwkv_v1=rwkv_v1, rwkv_v2=rwkv_v2,
             rwkv_g1=rwkv_g1, rwkv_g2=rwkv_g2, rwkv_k_k=rwkv_k_k, rwkv_k_a=rwkv_k_a, rwkv_r_k=rwkv_r_k,
             rwkv_lnx_w=rwkv_lnx_w, rwkv_lnx_b=rwkv_lnx_b, rwkv_w_o=rwkv_w_o,
             kv_norm_g=kv_norm_g, kv_ada_w=kv_ada_w, kv_ada_b=kv_ada_b, w_kv=w_kv,
             nsa_w_qg=nsa_w_qg, nsa_w_o=nsa_w_o,
             moe_w_grp=moe_w_grp, moe_b_grp=moe_b_grp, moe_w_exp=moe_w_exp, moe_b_exp=moe_b_exp,
             moe_w_gate=moe_w_gate, moe_w_up=moe_w_up, moe_w_down=moe_w_down)

    def make_prompt_attend(kv):
        ctx = branch_context(kv[:, :, :4], cmp_pos, cmp_w1, cmp_b1, cmp_w2)
        kvw_pad = jnp.pad(kv[:, :, 4:], ((0, 0), (WINDOW, 0), (0, 0), (0, 0), (0, 0)))
        return lambda q, g: prompt_attend(q, g, ctx, kvw_pad)

    bp = x_prompt.shape[0]
    y_prompt, kv_p, shift_prompt, wkv_prompt = run_trunk(
        x_prompt, c_prompt,
        jnp.zeros((N_A, bp, D_MODEL), x_prompt.dtype),
        jnp.zeros((N_A, bp, RWKV_HEADS, RWKV_HEAD_DIM, RWKV_HEAD_DIM), x_prompt.dtype),
        make_prompt_attend, P)

    db, n_pages = page_table.shape
    past_len = n_pages * cache_nsa_kv.shape[1]
    past = cache_nsa_kv[page_table].reshape(db, past_len, 4, NSA_KV_HEADS, NSA_HEAD_DIM)
    wb = cache_win_kv.shape[1]

    def make_sample_attend(kv):
        S = kv.shape[1]
        ctx = branch_context(jnp.concatenate([past, kv[:, :, :4]], axis=1), cmp_pos, cmp_w1, cmp_b1, cmp_w2)
        kvw = jnp.concatenate([cache_win_kv, kv[:, :, 4:]], axis=1)
        q_pos = past_len + jnp.arange(S, dtype=jnp.int32)
        w_pos = past_len - wb + jnp.arange(wb + S, dtype=jnp.int32)
        return lambda q, g: nsa_attend(q, g, q_pos, *ctx, kvw[:, :, 0], kvw[:, :, 1], w_pos)

    y_sample, kv_s, shift_sample, wkv_sample = run_trunk(
        x_sample, c_sample, state_shift, state_wkv, make_sample_attend, P)

    kv_prompt = kv_p[:, :, :4]
    kv_sample = kv_s[:, :, :4]
    win_prompt = kv_p[:, -min(WINDOW, kv_p.shape[1]):, 4:]
    win_all = jnp.concatenate([cache_win_kv, kv_s[:, :, 4:]], axis=1)
    win_sample = win_all[:, -min(WINDOW, win_all.shape[1]):]
    return (y_prompt, y_sample, kv_prompt, kv_sample, win_prompt, win_sample, wkv_prompt, wkv_sample, shift_prompt, shift_sample)
```

```python
import functools
import math

import jax
import jax.numpy as jnp
from jax import lax
from jax.experimental import pallas as pl
from jax.experimental.pallas import tpu as pltpu

D_MODEL = 1024
DEPTH = 4
N_A = 2
RWKV_HEADS = 16
RWKV_HEAD_DIM = 64
NSA_HEADS = 16
NSA_KV_HEADS = 4
NSA_HEAD_DIM = 64
NSA_GROUP = 4
CMP_STRIDE = 16
CMP_LEN = 32
CMP_HIDDEN = 256
SEL_BLOCK = 64
SEL_TOP = 16
WINDOW = 512
N_GROUPS = 4
GROUP_EXPERTS = 4
N_EXPERTS = 16
EXPERT_FF = 512
NORM_EPS = 1e-6
LNX_EPS = 64e-5
NEG = -1e30

LANES = 128
SUBLANES = 8
VMEM_LIMIT = 56 * 1024 * 1024
BF16 = jnp.bfloat16
F32 = jnp.float32


def _cp(*sem):
    return pltpu.CompilerParams(dimension_semantics=sem, vmem_limit_bytes=VMEM_LIMIT)


def _dot(a, b):
    return jnp.dot(a, b, preferred_element_type=F32)


def _dot_nt(a, b):
    return lax.dot_general(a, b, (((1,), (1,)), ((), ())), preferred_element_type=F32)


def _split2(x):
    hi = x.astype(BF16)
    lo = (x - hi.astype(F32)).astype(BF16)
    return hi, lo


def _dot_x2(x, w_bf16):
    hi, lo = _split2(x)
    return _dot(hi, w_bf16) + _dot(lo, w_bf16)


def _dot_3pass(x, w):
    xh, xl = _split2(x)
    wh, wl = _split2(w)
    return _dot(xh, wh) + (_dot(xh, wl) + _dot(xl, wh))


def _iota(shape, dim):
    return lax.broadcasted_iota(jnp.int32, shape, dim)


def _ones_bd():
    return (_iota((LANES, LANES), 0) // 64 == _iota((LANES, LANES), 1) // 64).astype(BF16)


def _seg_sum64(x, ones_bd):
    cols = [_dot_x2(x[:, s * LANES:(s + 1) * LANES], ones_bd) for s in range(x.shape[1] // LANES)]
    return jnp.concatenate(cols, axis=1)


def _silu(x):
    return x * jax.nn.sigmoid(x)


def _norm_mod(x, g, sh, sc):
    y = x * lax.rsqrt(jnp.mean(x * x, axis=-1, keepdims=True) + NORM_EPS)
    return (y * g) * (1.0 + sc) + sh


def _mod_spec(vec, t_len, tm):
    bg, d = vec.shape
    if t_len % tm == 0:
        tpb = t_len // tm
        return vec.reshape(bg, 1, d), pl.BlockSpec((1, 1, d), lambda i, *_: (i // tpb, 0, 0))
    assert tm % t_len == 0
    n = bg * t_len
    return jnp.repeat(vec, t_len, axis=0).reshape(n // tm, tm, d), pl.BlockSpec((1, tm, d), lambda i, *_: (i, 0, 0))


def _full_spec(arr):
    nd = arr.ndim
    return pl.BlockSpec(arr.shape, lambda *_: (0,) * nd)


def _tile_rows(n, pref):
    tm = min(pref, n)
    assert n % tm == 0
    return tm


def _ada_body(c_ref, w_ref, b_ref, o_ref):
    o_ref[0] = _dot_3pass(_silu(c_ref[...]), w_ref[0]) + b_ref[0]


def ada_project(c, w, b, tn):
    nl, d, n = w.shape
    m = c.shape[0]
    return pl.pallas_call(
        _ada_body,
        out_shape=jax.ShapeDtypeStruct((nl, m, n), F32),
        grid=(nl, n // tn),
        in_specs=[pl.BlockSpec((m, d), lambda l, j: (0, 0)),
                  pl.BlockSpec((1, d, tn), lambda l, j: (l, 0, j)),
                  pl.BlockSpec((1, 1, tn), lambda l, j: (l, 0, j))],
        out_specs=pl.BlockSpec((1, m, tn), lambda l, j: (l, 0, j)),
        compiler_params=_cp("arbitrary", "arbitrary"),
        name="ada_project",
    )(c, w, b.reshape(nl, 1, n))


def _normmod_body(x_ref, g_ref, sh_ref, sc_ref, h_ref):
    h_ref[...] = _norm_mod(x_ref[...], g_ref[...], sh_ref[0], sc_ref[0])


def norm_mod(x, g, sh, sc, t_len):
    n, d = x.shape
    tm = _tile_rows(n, 512)
    sh_a, sh_s = _mod_spec(sh, t_len, tm)
    sc_a, sc_s = _mod_spec(sc, t_len, tm)
    return pl.pallas_call(
        _normmod_body,
        out_shape=jax.ShapeDtypeStruct((n, d), F32),
        grid=(n // tm,),
        in_specs=[pl.BlockSpec((tm, d), lambda i: (i, 0)), pl.BlockSpec((1, d), lambda i: (0, 0)), sh_s, sc_s],
        out_specs=pl.BlockSpec((tm, d), lambda i: (i, 0)),
        compiler_params=_cp("arbitrary"),
        name="norm_mod",
    )(x, g.reshape(1, d), sh_a, sc_a)


def _rmsnorm_body(x_ref, g_ref, o_ref):
    x = x_ref[...]
    o_ref[...] = x * lax.rsqrt(jnp.mean(x * x, axis=-1, keepdims=True) + NORM_EPS) * g_ref[...]


def rms_norm_final(x, g):
    n, d = x.shape
    tm = _tile_rows(n, 512)
    return pl.pallas_call(
        _rmsnorm_body,
        out_shape=jax.ShapeDtypeStruct((n, d), F32),
        grid=(n // tm,),
        in_specs=[pl.BlockSpec((tm, d), lambda i: (i, 0)), pl.BlockSpec((1, d), lambda i: (0, 0))],
        out_specs=pl.BlockSpec((tm, d), lambda i: (i, 0)),
        compiler_params=_cp("arbitrary"),
        name="rms_norm_final",
    )(x, g.reshape(1, d))


def _softplus(z):
    return jnp.maximum(z, 0.0) + jnp.log(1.0 + jnp.exp(-jnp.abs(z)))


def _rwkv_proj_body(has_vres, *refs):
    (h_ref, hp_ref, mix_ref, wrkv_ref, w0_ref, w1_ref, w2_ref, a0_ref, a1_ref, a2_ref,
     g1_ref, g2_ref, kk_ref, ka_ref) = refs[:14]
    refs = refs[14:]
    if has_vres:
        v0_ref, v1_ref, v2_ref, vf_ref = refs[:4]
        refs = refs[4:]
    r_out, w_out, k_out, v_out, a_out, b_out, g_out = refs

    h = h_ref[...]
    xx = hp_ref[...] - h
    mix = mix_ref[...]

    def xm(s):
        return (h + xx * mix[s:s + 1]).astype(BF16)

    r = _dot(xm(0), wrkv_ref[0])
    k = _dot(xm(1), wrkv_ref[1])
    x2 = xm(2)
    v = _dot(x2, wrkv_ref[2])
    wl = w0_ref[...] + _dot(jnp.tanh(_dot(xm(3), w1_ref[...])).astype(BF16), w2_ref[...])
    decay = jnp.exp(-jnp.exp(-_softplus(-wl) - 0.5))
    if has_vres:
        mv = jax.nn.sigmoid(v0_ref[...] + _dot(_dot(x2, v1_ref[...]).astype(BF16), v2_ref[...]))
        v = v + (vf_ref[...] - v) * mv
    a = jax.nn.sigmoid(a0_ref[...] + _dot(_dot(xm(4), a1_ref[...]).astype(BF16), a2_ref[...]))
    g = _dot(jax.nn.sigmoid(_dot(xm(5), g1_ref[...])).astype(BF16), g2_ref[...])
    kkr = k * kk_ref[...]
    ssq = _seg_sum64(kkr * kkr, _ones_bd())
    kk = kkr * lax.rsqrt(jnp.maximum(ssq, 1e-24))
    r_out[...] = r
    w_out[...] = decay
    k_out[...] = k * (1.0 + (a - 1.0) * ka_ref[...])
    v_out[...] = v
    a_out[...] = -kk
    b_out[...] = kk * a
    g_out[...] = g


def _pad_to(x, axis, size):
    pad = [(0, 0)] * x.ndim
    pad[axis] = (0, size - x.shape[axis])
    return jnp.pad(x, pad)


def rwkv_project(h, hp, P, l, v_first):
    n, d = h.shape
    tm = _tile_rows(n, 256)
    has_vres = v_first is not None
    row = lambda z: z.reshape(1, d)
    lora_in = lambda w: _pad_to(w, 1, -(-w.shape[1] // LANES) * LANES).astype(BF16)
    lora_out = lambda w: _pad_to(w, 0, -(-w.shape[0] // LANES) * LANES).astype(BF16)
    consts = [_pad_to(P['rwkv_mix'][l], 0, SUBLANES), P['rwkv_w_rkv'][l].astype(BF16),
              row(P['rwkv_w0'][l]), lora_in(P['rwkv_w1'][l]), lora_out(P['rwkv_w2'][l]),
              row(P['rwkv_a0'][l]), lora_in(P['rwkv_a1'][l]), lora_out(P['rwkv_a2'][l]),
              lora_in(P['rwkv_g1'][l]), lora_out(P['rwkv_g2'][l]),
              row(P['rwkv_k_k'][l]), row(P['rwkv_k_a'][l])]
    if has_vres:
        consts += [row(P['rwkv_v0'][l - 1]), lora_in(P['rwkv_v1'][l - 1]), lora_out(P['rwkv_v2'][l - 1])]
    tile = pl.BlockSpec((tm, d), lambda i: (i, 0))
    ins = [h, hp] + consts + ([v_first] if has_vres else [])
    in_specs = [tile, tile] + [_full_spec(c) for c in consts] + ([tile] if has_vres else [])
    return pl.pallas_call(
        functools.partial(_rwkv_proj_body, has_vres),
        out_shape=[jax.ShapeDtypeStruct((n, d), F32)] * 7,
        grid=(n // tm,),
        in_specs=in_specs,
        out_specs=[tile] * 7,
        compiler_params=_cp("arbitrary"),
        name="rwkv_project",
    )(*ins)


def _wkv_body(bb_n, tc_n, r_ref, w_ref, k_ref, v_ref, a_ref, b_ref, s0_ref, y_ref, s_ref):
    tc = pl.program_id(1)

    @pl.when(tc == 0)
    def _():
        s_ref[...] = s0_ref[...]

    n = RWKV_HEAD_DIM
    ones_bd = _ones_bd()
    diag = (_iota((n, LANES), 1) % n == _iota((n, LANES), 0)).astype(F32)
    sel = ((_iota((SUBLANES, LANES), 1) // n == _iota((SUBLANES, LANES), 0))).astype(BF16)

    def step(t, carry):
        for bb in range(bb_n):
            for p in range(SUBLANES):
                idx = bb * SUBLANES + p
                s = s_ref[idx]

                def row(ref):
                    return ref[bb, pl.ds(t, 1), p, :]

                sa = _dot((s * row(a_ref)).astype(BF16), ones_bd)
                vb = _dot((diag * row(v_ref)).astype(BF16), ones_bd)
                s = s * row(w_ref) + sa * row(b_ref) + vb * row(k_ref)
                s_ref[idx] = s
                y2 = _dot_nt(sel, (s * row(r_ref)).astype(BF16))
                y_ref[bb, pl.ds(t, 1), pl.ds(2 * p, 2), :] = y2[0:2].reshape(1, 2, n)
        return carry

    lax.fori_loop(0, tc_n, step, 0)


def wkv_scan(r, w, k, v, a, b, s0p, bsz, t_len, bb_n, tc_n):
    d = D_MODEL
    shp = lambda z: z.reshape(bsz, t_len, SUBLANES, LANES)
    tok = pl.BlockSpec((bb_n, tc_n, SUBLANES, LANES), lambda i, j: (i, j, 0, 0))
    st = pl.BlockSpec((bb_n * SUBLANES, RWKV_HEAD_DIM, LANES), lambda i, j: (i, 0, 0))
    y, s_t = pl.pallas_call(
        functools.partial(_wkv_body, bb_n, tc_n),
        out_shape=[jax.ShapeDtypeStruct((bsz, t_len, RWKV_HEADS, RWKV_HEAD_DIM), F32),
                   jax.ShapeDtypeStruct(s0p.shape, F32)],
        grid=(bsz // bb_n, t_len // tc_n),
        in_specs=[tok] * 6 + [st],
        out_specs=[pl.BlockSpec((bb_n, tc_n, RWKV_HEADS, RWKV_HEAD_DIM), lambda i, j: (i, j, 0, 0)), st],
        compiler_params=_cp("arbitrary", "arbitrary"),
        name="wkv_scan",
    )(shp(r), shp(w), shp(k), shp(v), shp(a), shp(b), s0p)
    return y.reshape(bsz * t_len, d), s_t


def pack_state(s):
    bsz = s.shape[0]
    n = RWKV_HEAD_DIM
    return s.reshape(bsz, SUBLANES, 2, n, n).transpose(0, 1, 3, 2, 4).reshape(bsz * SUBLANES, n, 2 * n)


def unpack_state(sp, bsz):
    n = RWKV_HEAD_DIM
    return sp.reshape(bsz, SUBLANES, n, 2, n).transpose(0, 1, 3, 2, 4).reshape(bsz, RWKV_HEADS, n, n)


def _rwkv_out_body(y_ref, r_ref, k_ref, v_ref, g_ref, x_ref, gt_ref, lw_ref, lb_ref, rk_ref, wo_ref, o_ref):
    ones_bd = _ones_bd()
    y = y_ref[...]
    inv_n = 1.0 / RWKV_HEAD_DIM
    mu = _seg_sum64(y, ones_bd) * inv_n
    yc = y - mu
    var = _seg_sum64(yc * yc, ones_bd) * inv_n
    yn = yc * lax.rsqrt(var + LNX_EPS) * lw_ref[...] + lb_ref[...]
    bonus = _seg_sum64(r_ref[...] * k_ref[...] * rk_ref[...], ones_bd) * v_ref[...]
    out = _dot(((yn + bonus) * g_ref[...]).astype(BF16), wo_ref[...])
    o_ref[...] = x_ref[...] + gt_ref[0] * out


def rwkv_output(y, r, k, v, g, x, gt, P, l, t_len):
    n, d = x.shape
    tm = _tile_rows(n, 256)
    gt_a, gt_s = _mod_spec(gt, t_len, tm)
    row = lambda z: z.reshape(1, d)
    consts = [row(P['rwkv_lnx_w'][l]), row(P['rwkv_lnx_b'][l]), row(P['rwkv_r_k'][l]), P['rwkv_w_o'][l].astype(BF16)]
    tile = pl.BlockSpec((tm, d), lambda i: (i, 0))
    return pl.pallas_call(
        _rwkv_out_body,
        out_shape=jax.ShapeDtypeStruct((n, d), F32),
        grid=(n // tm,),
        in_specs=[tile] * 6 + [gt_s] + [_full_spec(c) for c in consts],
        out_specs=tile,
        compiler_params=_cp("arbitrary"),
        name="rwkv_output",
    )(y, r, k, v, g, x, gt_a, *consts)


BIGNEG = -3.0e38


def _route(logits):
    lane = _iota(logits.shape, 1).astype(F32)
    gmask = lane < N_GROUPS
    gmax = jnp.max(jnp.where(gmask, logits, BIGNEG), axis=-1, keepdims=True)
    grp = jnp.min(jnp.where(gmask & (logits == gmax), lane, 999.0), axis=-1, keepdims=True)
    denom = jnp.sum(jnp.where(gmask, jnp.exp(jnp.where(gmask, logits - gmax, 0.0)), 0.0), axis=-1, keepdims=True)
    p_grp = 1.0 / denom
    lo = N_GROUPS + GROUP_EXPERTS * grp
    inmask = (lane >= lo) & (lane < lo + GROUP_EXPERTS)
    m1 = jnp.max(jnp.where(inmask, logits, BIGNEG), axis=-1, keepdims=True)
    i1 = jnp.min(jnp.where(inmask & (logits == m1), lane, 999.0), axis=-1, keepdims=True)
    mask2 = inmask & (lane != i1)
    m2 = jnp.max(jnp.where(mask2, logits, BIGNEG), axis=-1, keepdims=True)
    i2 = jnp.min(jnp.where(mask2 & (logits == m2), lane, 999.0), axis=-1, keepdims=True)
    e12 = jnp.exp(m2 - m1)
    w1 = p_grp / (1.0 + e12)
    w2 = p_grp * e12 / (1.0 + e12)
    return jnp.where(lane == 0, i1 - N_GROUPS,
                     jnp.where(lane == 1, i2 - N_GROUPS, jnp.where(lane == 2, w1, jnp.where(lane == 3, w2, 0.0))))


def _moe_body(x_ref, g_ref, sh_ref, sc_ref, gt_ref, wr_ref, br_ref, wg_ref, wu_ref, wd_ref, o_ref,
              h_scr, route_scr, acc_scr):
    e = pl.program_id(1)

    @pl.when(e == 0)
    def _():
        h = _norm_mod(x_ref[...], g_ref[...], sh_ref[0], sc_ref[0])
        h_scr[...] = h.astype(BF16)
        route_scr[...] = _route(_dot_3pass(h, wr_ref[...]) + br_ref[...])
        acc_scr[...] = jnp.zeros_like(acc_scr)

    h = h_scr[...]
    act = _silu(_dot(h, wg_ref[0].astype(BF16))) * _dot(h, wu_ref[0].astype(BF16))
    rt = route_scr[...]
    ef = e.astype(F32)
    coef = jnp.where(rt[:, 0:1] == ef, rt[:, 2:3], 0.0) + jnp.where(rt[:, 1:2] == ef, rt[:, 3:4], 0.0)
    acc_scr[...] += _dot((act * coef).astype(BF16), wd_ref[0].astype(BF16))

    @pl.when(e == N_EXPERTS - 1)
    def _():
        o_ref[...] = x_ref[...] + gt_ref[0] * acc_scr[...]


def moe_dense(x, sh, sc, gt, P, l, t_len):
    n, d = x.shape
    tm = _tile_rows(n, 1024)
    mods = [_mod_spec(z, t_len, tm) for z in (sh, sc, gt)]
    wr = _pad_to(jnp.concatenate([P['moe_w_grp'][l], P['moe_w_exp'][l]], axis=1), 1, LANES)
    br = _pad_to(jnp.concatenate([P['moe_b_grp'][l], P['moe_b_exp'][l]]), 0, LANES).reshape(1, LANES)
    tile = pl.BlockSpec((tm, d), lambda i, e: (i, 0))
    ff = EXPERT_FF
    return pl.pallas_call(
        _moe_body,
        out_shape=jax.ShapeDtypeStruct((n, d), F32),
        grid=(n // tm, N_EXPERTS),
        in_specs=[tile, pl.BlockSpec((1, d), lambda i, e: (0, 0))] + [m[1] for m in mods]
        + [pl.BlockSpec((d, LANES), lambda i, e: (0, 0)), pl.BlockSpec((1, LANES), lambda i, e: (0, 0)),
           pl.BlockSpec((1, d, ff), lambda i, e: (e, 0, 0)), pl.BlockSpec((1, d, ff), lambda i, e: (e, 0, 0)),
           pl.BlockSpec((1, ff, d), lambda i, e: (e, 0, 0))],
        out_specs=tile,
        scratch_shapes=[pltpu.VMEM((tm, d), BF16), pltpu.VMEM((tm, LANES), F32), pltpu.VMEM((tm, d), F32)],
        compiler_params=_cp("arbitrary", "arbitrary"),
        name="moe_dense",
    )(x, P['norm_ffn_g'][l].reshape(1, d), *[m[0] for m in mods], wr, br,
      P['moe_w_gate'][l], P['moe_w_up'][l], P['moe_w_down'][l])


def _nm_matmul_body(x_ref, g_ref, sh_ref, sc_ref, w_ref, o_ref):
    h = _norm_mod(x_ref[...], g_ref[...], sh_ref[0], sc_ref[0])
    o_ref[...] = _dot(h.astype(BF16), w_ref[...])


def norm_mod_matmul(x, g, sh, sc, w, t_len):
    n, d = x.shape
    nout = w.shape[1]
    tm = _tile_rows(n, 512)
    sh_a, sh_s = _mod_spec(sh, t_len, tm)
    sc_a, sc_s = _mod_spec(sc, t_len, tm)
    return pl.pallas_call(
        _nm_matmul_body,
        out_shape=jax.ShapeDtypeStruct((n, nout), F32),
        grid=(n // tm,),
        in_specs=[pl.BlockSpec((tm, d), lambda i: (i, 0)), pl.BlockSpec((1, d), lambda i: (0, 0)), sh_s, sc_s,
                  pl.BlockSpec((d, nout), lambda i: (0, 0))],
        out_specs=pl.BlockSpec((tm, nout), lambda i: (i, 0)),
        compiler_params=_cp("arbitrary"),
        name="norm_mod_matmul",
    )(x, g.reshape(1, d), sh_a, sc_a, w.astype(BF16))


def _gelu_tanh(x):
    return 0.5 * x * (1.0 + jnp.tanh(math.sqrt(2.0 / math.pi) * (x + 0.044715 * (x * x * x))))


def _compress_body(n_blk, rows_per_blk, slabs, n_prefetch, *refs):
    refs = refs[n_prefetch:]
    row_refs = refs[:n_blk]
    pos_ref, w1_ref, b1_ref, w2_ref, o_ref, x_scr = refs[n_blk:]
    hd = NSA_HEAD_DIM
    cpb = rows_per_blk // CMP_STRIDE
    nch = n_blk * cpb
    for z in range(2):
        for j in range(n_blk):
            for gp in range(NSA_KV_HEADS // 2):
                for s in range(CMP_STRIDE):
                    src = row_refs[j][0, pl.ds(s * slabs + z * 2 + gp, cpb, stride=CMP_STRIDE * slabs), :]
                    for gl in range(2):
                        g = 2 * gp + gl
                        x_scr[g * nch + j * cpb:g * nch + (j + 1) * cpb, s * hd:(s + 1) * hd] = (
                            src[:, gl * hd:(gl + 1) * hd])
        w1 = w1_ref[z]
        part = _dot(x_scr[...].astype(BF16), w1)
        pp = _dot_x2(pos_ref[z], w1)
        posb = pp[0:1, :CMP_HIDDEN] + pp[1:2, CMP_HIDDEN:] + b1_ref[z]
        w2 = w2_ref[z]
        for g in range(NSA_KV_HEADS):
            pg = part[g * nch:(g + 1) * nch]
            nxt = pltpu.roll(pg[:, CMP_HIDDEN:], nch - 1, 0)
            hid = posb + pg[:, :CMP_HIDDEN] + nxt
            o_ref[z, 0, g] = _dot(_gelu_tanh(hid).astype(BF16), w2)


def _compress_consts(cmp_pos, cmp_w1, cmp_b1, cmp_w2):
    hd = NSA_HEAD_DIM
    k_half = CMP_STRIDE * hd
    w1 = cmp_w1.reshape(2, 2, k_half, CMP_HIDDEN)
    w1cat = jnp.concatenate([w1[:, 0], w1[:, 1]], axis=-1).astype(BF16)
    pos = _pad_to(cmp_pos.reshape(2, 2, k_half), 1, SUBLANES)
    return pos, w1cat, cmp_b1.reshape(2, 1, CMP_HIDDEN), cmp_w2.astype(BF16)


def compress_call(row_arrays, row_specs, grid, prefetch, bsz, nch, rows_per_blk, slabs, consts):
    n_blk = len(row_arrays)
    np_ = 1 if prefetch is not None else 0
    cspecs = [pl.BlockSpec(c.shape, (lambda *_, nd=c.ndim: (0,) * nd)) for c in consts]
    gs = pltpu.PrefetchScalarGridSpec(
        num_scalar_prefetch=np_, grid=grid, in_specs=list(row_specs) + cspecs,
        out_specs=pl.BlockSpec((2, 1, NSA_KV_HEADS, nch, NSA_HEAD_DIM), lambda b, *_: (0, b, 0, 0, 0)),
        scratch_shapes=[pltpu.VMEM((NSA_KV_HEADS * nch, CMP_STRIDE * NSA_HEAD_DIM), F32)])
    args = ([prefetch] if np_ else []) + list(row_arrays) + list(consts)
    return pl.pallas_call(
        functools.partial(_compress_body, n_blk, rows_per_blk, slabs, np_),
        out_shape=jax.ShapeDtypeStruct((2, bsz, NSA_KV_HEADS, nch, NSA_HEAD_DIM), F32),
        grid_spec=gs, compiler_params=_cp("arbitrary"), name="nsa_compress",
    )(*args)


def _masked_softmax(s, mask):
    s = jnp.where(mask, s, NEG)
    e = jnp.exp(s - jnp.max(s, axis=-1, keepdims=True))
    return jnp.where(mask, e / jnp.sum(e, axis=-1, keepdims=True), 0.0)


def _overlap(nch, nbp):
    n = _iota((nch, nbp), 0) * CMP_STRIDE
    j = _iota((nch, nbp), 1) * SEL_BLOCK
    return ((n <= j + SEL_BLOCK - 1) & (n + CMP_LEN - 1 >= j)).astype(BF16)


def _compressed_branch(qs, kc, vc, slope, trow, nc_valid, n_tok):
    nch = kc.shape[0]
    ncol = _iota((1, nch), 1)
    d_c = trow - (ncol * CMP_STRIDE + (CMP_LEN - 1))
    m_c = (d_c >= 0) & (ncol < nc_valid)
    s_c = _dot_nt(qs, kc.astype(BF16)) - slope * d_c.astype(F32)
    p_c = _masked_softmax(s_c, m_c)
    o_c = _dot(p_c.astype(BF16), vc.astype(BF16))
    psum = p_c[0:n_tok]
    for p in range(1, NSA_GROUP):
        psum = psum + p_c[p * n_tok:(p + 1) * n_tok]
    return o_c, psum


def _select_blocks(imp, tpos, nb):
    nbp = imp.shape[1]
    blk = _iota((1, nbp), 1)
    cur = tpos // SEL_BLOCK
    valid = (blk * SEL_BLOCK <= tpos) & (blk < nb)
    forced = (blk == 0) | (blk == cur) | (blk == cur - 1)
    score = jnp.where(valid, jnp.where(forced, -NEG, imp), NEG)
    score = jnp.where(blk < nb, score, 2.0 * NEG)
    rank = jnp.zeros(imp.shape, F32)
    for i in range(nb):
        ci = score[:, i:i + 1]
        rank = rank + jnp.where((ci > score) | ((ci == score) & (blk > i)), 1.0, 0.0)
    return jnp.where(rank < float(min(SEL_TOP, nb)), 1.0, 0.0)


def _expand_sel(sel, k0, tk):
    nbp = sel.shape[1]
    e_mat = (_iota((nbp, tk), 0) == (k0 + _iota((nbp, tk), 1)) // SEL_BLOCK).astype(BF16)
    m = _dot(sel.astype(BF16), e_mat)
    return jnp.concatenate([m] * NSA_GROUP, axis=0)


def _gate_combine(gate, branches, n_tok):
    outs = []
    for p in range(NSA_GROUP):
        acc = None
        for c, o in enumerate(branches):
            term = gate[:, p * 3 + c:p * 3 + c + 1] * o[p * n_tok:(p + 1) * n_tok]
            acc = term if acc is None else acc + term
        outs.append(acc)
    return jnp.concatenate(outs, axis=1)


def _stack_heads(qg, scale):
    hd = NSA_HEAD_DIM
    return jnp.concatenate([qg[:, p * hd:(p + 1) * hd] for p in range(NSA_GROUP)], axis=0) * scale


def _nsa_prompt_body(tq, tk, t_len, q_ref, gate_ref, slope_ref, kc_ref, vc_ref,
                     ks_ref, vs_ref, kw_ref, vw_ref, o_ref):
    qi = pl.program_id(2)
    q0 = qi * tq
    rows = NSA_GROUP * tq
    nch = t_len // CMP_STRIDE
    nb = t_len // SEL_BLOCK
    qs = _stack_heads(q_ref[0], NSA_HEAD_DIM ** -0.5).astype(BF16)
    slope = slope_ref[0]
    trow = q0 + _iota((rows, 1), 0) % tq
    tpos = q0 + _iota((tq, 1), 0)

    o_c, psum = _compressed_branch(qs, kc_ref[0, 0, 0], vc_ref[0, 0, 0], slope, trow, nch - 1, tq)
    imp = _dot_x2(psum, _overlap(nch, nb))
    sel = _select_blocks(imp, tpos, nb)

    def flash(k_ref, v_ref, lo, hi, selected):
        def body(kt, carry):
            m, l, acc = carry
            k0 = pl.multiple_of(kt * tk, tk)
            s = _dot_nt(qs, k_ref[0, 0, 0, pl.ds(k0, tk), :])
            d = trow - (k0 + _iota((1, tk), 1))
            s = s - slope * d.astype(F32)
            if selected:
                mask = (d >= 0) & (_expand_sel(sel, k0, tk) > 0.5)
            else:
                mask = (d >= 0) & (d < WINDOW)
            s = jnp.where(mask, s, NEG)
            m_new = jnp.maximum(m, jnp.max(s, axis=-1, keepdims=True))
            alpha = jnp.exp(m - m_new)
            p = jnp.where(mask, jnp.exp(s - m_new), 0.0)
            l = alpha * l + jnp.sum(p, axis=-1, keepdims=True)
            acc = alpha * acc + _dot(p.astype(BF16), v_ref[0, 0, 0, pl.ds(k0, tk), :])
            return m_new, l, acc

        init = (jnp.full((rows, 1), NEG, F32), jnp.zeros((rows, 1), F32), jnp.zeros((rows, NSA_HEAD_DIM), F32))
        _, l, acc = lax.fori_loop(lo, hi, body, init)
        return acc * jnp.where(l > 0.0, 1.0 / l, 0.0)

    hi = (q0 + tq - 1) // tk + 1
    o_s = flash(ks_ref, vs_ref, 0, hi, True)
    o_w = flash(kw_ref, vw_ref, jnp.maximum(q0 - (WINDOW - 1), 0) // tk, hi, False)
    gate = jax.nn.sigmoid(gate_ref[0, 0])
    o_ref[0] = _gate_combine(gate, (o_c, o_s, o_w), tq)


def _alibi_slope_rows(n_tok):
    h = jnp.arange(1, NSA_HEADS + 1, dtype=F32)
    sl = jnp.exp2(-8.0 * h / NSA_HEADS).reshape(NSA_KV_HEADS, NSA_GROUP, 1)
    return jnp.broadcast_to(sl, (NSA_KV_HEADS, NSA_GROUP, n_tok)).reshape(NSA_KV_HEADS, NSA_GROUP * n_tok, 1)


def _gate_array(qg, bsz, t_len):
    gates = qg[:, NSA_HEADS * NSA_HEAD_DIM:NSA_HEADS * NSA_HEAD_DIM + 3 * NSA_HEADS]
    gates = gates.reshape(bsz, t_len, NSA_KV_HEADS, 3 * NSA_GROUP).transpose(0, 2, 1, 3)
    return _pad_to(gates, 3, LANES)


def nsa_prompt_attend(qg, cmp_kv, kvt, bsz, t_len):
    tq = min(128, t_len)
    tk = min(512, t_len)
    gq = NSA_GROUP * NSA_HEAD_DIM
    nch = cmp_kv.shape[3]
    q3 = qg.reshape(bsz, t_len, qg.shape[1])
    full_t = lambda z: pl.BlockSpec((1, 1, 1, t_len, NSA_HEAD_DIM), lambda b, g, i, z=z: (z, b, g, 0, 0))
    cmp_spec = lambda z: pl.BlockSpec((1, 1, 1, nch, NSA_HEAD_DIM), lambda b, g, i, z=z: (z, b, g, 0, 0))

    out = pl.pallas_call(
        functools.partial(_nsa_prompt_body, tq, tk, t_len),
        out_shape=jax.ShapeDtypeStruct((bsz, t_len, NSA_HEADS * NSA_HEAD_DIM), F32),
        grid=(bsz, NSA_KV_HEADS, t_len // tq),
        in_specs=[pl.BlockSpec((1, tq, gq), lambda b, g, i: (b, i, g)),
                  pl.BlockSpec((1, 1, tq, LANES), lambda b, g, i: (b, g, i, 0)),
                  pl.BlockSpec((1, NSA_GROUP * tq, 1), lambda b, g, i: (g, 0, 0)),
                  cmp_spec(0), cmp_spec(1), full_t(2), full_t(3), full_t(4), full_t(5)],
        out_specs=pl.BlockSpec((1, tq, gq), lambda b, g, i: (b, i, g)),
        compiler_params=_cp("arbitrary", "arbitrary", "arbitrary"),
        name="nsa_prompt_attend",
    )(q3, _gate_array(qg, bsz, t_len), _alibi_slope_rows(tq), cmp_kv, cmp_kv, kvt, kvt, kvt, kvt)
    return out.reshape(bsz * t_len, NSA_HEADS * NSA_HEAD_DIM)


def _nsa_sample_body(n_pages, page_rows, s_len, wb, pt_ref, q_ref, gate_ref, slope_ref, kc_ref, vc_ref, *refs):
    page_refs = refs[:n_pages]
    new_ref, win_ref, o_ref = refs[n_pages:]
    del pt_ref
    hd = NSA_HEAD_DIM
    gq = NSA_GROUP * hd
    past_len = n_pages * page_rows
    rows = NSA_GROUP * s_len
    nch = kc_ref.shape[3]
    nb = -(-(past_len + s_len) // SEL_BLOCK)
    nbp = 64
    assert nb <= nbp and page_rows % SEL_BLOCK == 0
    trow = past_len + _iota((rows, 1), 0) % s_len
    tpos = past_len + _iota((s_len, 1), 0)
    tloc = _iota((rows, 1), 0) % s_len
    q_all = q_ref[0]
    new = new_ref[0]
    outs = []
    for g in range(NSA_KV_HEADS):
        qs = _stack_heads(q_all[:, g * gq:(g + 1) * gq], hd ** -0.5)
        qsb = qs.astype(BF16)
        slope = slope_ref[g]
        o_c, psum = _compressed_branch(qsb, kc_ref[0, 0, g], vc_ref[0, 0, g], slope, trow, nch - 1, s_len)
        imp = _dot_x2(psum, _overlap(nch, nbp))
        sel = _select_blocks(imp, tpos, nb)

        def new_scores(z):
            knew = new[:, z * gq + g * hd:z * gq + (g + 1) * hd]
            cols = []
            for t2 in range(s_len):
                sc = jnp.sum(qs * knew[t2:t2 + 1], axis=-1, keepdims=True)
                cols.append((sc - slope * (tloc - t2).astype(F32), tloc >= t2))
            return cols

        def finish(tiles, cols, vals, vnew):
            m = jnp.full((rows, 1), NEG, F32)
            for s, mk in tiles + cols:
                m = jnp.maximum(m, jnp.max(jnp.where(mk, s, NEG), axis=-1, keepdims=True))
            l = jnp.zeros((rows, 1), F32)
            acc = jnp.zeros((rows, hd), F32)
            for (s, mk), v in zip(tiles, vals):
                p = jnp.where(mk, jnp.exp(jnp.where(mk, s, NEG) - m), 0.0)
                l = l + jnp.sum(p, axis=-1, keepdims=True)
                acc = acc + _dot(p.astype(BF16), v.astype(BF16))
            for t2, (s, mk) in enumerate(cols):
                p = jnp.where(mk, jnp.exp(jnp.where(mk, s, NEG) - m), 0.0)
                l = l + p
                acc = acc + p * vnew[t2:t2 + 1]
            return acc * jnp.where(l > 0.0, 1.0 / l, 0.0)

        tiles, vals = [], []
        for j in range(n_pages):
            page = page_refs[j]
            k0 = j * page_rows
            s = _dot_nt(qsb, page[0, :, g * hd:(g + 1) * hd].astype(BF16))
            d = trow - (k0 + _iota((1, page_rows), 1))
            s = s - slope * d.astype(F32)
            tiles.append((s, (d >= 0) & (_expand_sel(sel, k0, page_rows) > 0.5)))
            vals.append(page[0, :, gq + g * hd:gq + (g + 1) * hd])
        cols = new_scores(2)
        blk_new = past_len // SEL_BLOCK
        sel_new = jnp.concatenate([sel[:, blk_new:blk_new + 1]] * NSA_GROUP, axis=0) > 0.5
        cols = [(s, mk & sel_new) for s, mk in cols]
        o_s = finish(tiles, cols, vals, new[:, 3 * gq + g * hd:3 * gq + (g + 1) * hd])

        s = _dot_nt(qsb, win_ref[0, :, g * hd:(g + 1) * hd].astype(BF16))
        d = trow - (past_len - wb + _iota((1, wb), 1))
        s = s - slope * d.astype(F32)
        cols = [(sc, mk) for sc, mk in new_scores(4)]
        o_w = finish([(s, (d >= 0) & (d < WINDOW))], cols, [win_ref[0, :, gq + g * hd:gq + (g + 1) * hd]],
                     new[:, 5 * gq + g * hd:5 * gq + (g + 1) * hd])
        outs.append(_gate_combine(jax.nn.sigmoid(gate_ref[0, g]), (o_c, o_s, o_w), s_len))
    o_ref[0] = jnp.concatenate(outs, axis=1)


def nsa_sample_attend(qg, cmp_kv, cache_pages, page_table, cache_win, kv_new, bsz, s_len):
    n_pages = page_table.shape[1]
    page_rows = cache_pages.shape[1]
    wb = cache_win.shape[1]
    gq = NSA_GROUP * NSA_HEAD_DIM
    half = 2 * NSA_KV_HEADS * NSA_HEAD_DIM
    nch = cmp_kv.shape[3]
    q3 = qg.reshape(bsz, s_len, qg.shape[1])
    cmp_spec = lambda z: pl.BlockSpec((1, 1, NSA_KV_HEADS, nch, NSA_HEAD_DIM), lambda b, pt, z=z: (z, b, 0, 0, 0))
    page_specs = [pl.BlockSpec((1, page_rows, half), lambda b, pt, j=j: (pt[b, j], 0, 1)) for j in range(n_pages)]
    gs = pltpu.PrefetchScalarGridSpec(
        num_scalar_prefetch=1, grid=(bsz,),
        in_specs=[pl.BlockSpec((1, s_len, q3.shape[2]), lambda b, pt: (b, 0, 0)),
                  pl.BlockSpec((1, NSA_KV_HEADS, s_len, LANES), lambda b, pt: (b, 0, 0, 0)),
                  pl.BlockSpec((NSA_KV_HEADS, NSA_GROUP * s_len, 1), lambda b, pt: (0, 0, 0)),
                  cmp_spec(0), cmp_spec(1)] + page_specs
        + [pl.BlockSpec((1, s_len, kv_new.shape[1]), lambda b, pt: (b, 0, 0)),
           pl.BlockSpec((1, wb, half), lambda b, pt: (b, 0, 0))],
        out_specs=pl.BlockSpec((1, s_len, NSA_HEADS * NSA_HEAD_DIM), lambda b, pt: (b, 0, 0)))
    out = pl.pallas_call(
        functools.partial(_nsa_sample_body, n_pages, page_rows, s_len, wb),
        out_shape=jax.ShapeDtypeStruct((bsz, s_len, NSA_HEADS * NSA_HEAD_DIM), F32),
        grid_spec=gs, compiler_params=_cp("arbitrary"), name="nsa_sample_attend",
    )(page_table, q3, _gate_array(qg, bsz, s_len), _alibi_slope_rows(s_len), cmp_kv, cmp_kv,
      *([cache_pages] * n_pages), kv_new.reshape(bsz, s_len, kv_new.shape[1]), cache_win)
    return out.reshape(bsz * s_len, NSA_HEADS * NSA_HEAD_DIM)


def _resid_matmul_body(o_ref, x_ref, gt_ref, w_ref, out_ref):
    out_ref[...] = x_ref[...] + gt_ref[0] * _dot(o_ref[...].astype(BF16), w_ref[...])


def resid_matmul(o, x, gt, w, t_len):
    n, d = x.shape
    tm = _tile_rows(n, 512)
    gt_a, gt_s = _mod_spec(gt, t_len, tm)
    tile = pl.BlockSpec((tm, d), lambda i: (i, 0))
    return pl.pallas_call(
        _resid_matmul_body,
        out_shape=jax.ShapeDtypeStruct((n, d), F32),
        grid=(n // tm,),
        in_specs=[pl.BlockSpec((tm, o.shape[1]), lambda i: (i, 0)), tile, gt_s, _full_spec(w)],
        out_specs=tile,
        compiler_params=_cp("arbitrary"),
        name="resid_matmul",
    )(o, x, gt_a, w.astype(BF16))


def _run_trunk(x, mods, kvmod, shift0, wkv0, P, bsz, t_len, make_attend, scan_bb, scan_tc):
    d = D_MODEL
    v_first, attend, kv = None, None, None
    shifts, states = [], []
    for l in range(DEPTH):
        sh1, sc1, gt1, sh2, sc2, gt2 = [mods[l][:, i * d:(i + 1) * d] for i in range(6)]
        if l < N_A:
            h = norm_mod(x, P['norm_mix_g'][l], sh1, sc1, t_len)
            h3 = h.reshape(bsz, t_len, d)
            hp = jnp.concatenate([shift0[l][:, None, :], h3[:, :-1]], axis=1).reshape(bsz * t_len, d)
            r, w, k, v, a, b, g = rwkv_project(h, hp, P, l, v_first)
            if v_first is None:
                v_first = v
            y, s_t = wkv_scan(r, w, k, v, a, b, pack_state(wkv0[l]), bsz, t_len, scan_bb, scan_tc)
            x = rwkv_output(y, r, k, v, g, x, gt1, P, l, t_len)
            shifts.append(h3[:, -1])
            states.append(unpack_state(s_t, bsz))
        else:
            w_qg = _pad_to(P['nsa_w_qg'][l - N_A], 1, 9 * LANES)
            qg = norm_mod_matmul(x, P['norm_mix_g'][l], sh1, sc1, w_qg, t_len)
            x = resid_matmul(attend(qg), x, gt1, P['nsa_w_o'][l - N_A], t_len)
        x = moe_dense(x, sh2, sc2, gt2, P, l, t_len)
        if l == N_A - 1:
            kv = norm_mod_matmul(x, P['kv_norm_g'], kvmod[:, :d], kvmod[:, d:], P['w_kv'], t_len)
            attend = make_attend(kv)
    return rms_norm_final(x, P['norm_f_g']), kv, jnp.stack(shifts), jnp.stack(states)


def kernel(x_prompt, x_sample, cache_nsa_kv, cache_win_kv, state_wkv, state_shift, page_table, c_prompt, c_sample, ada_w, ada_b, norm_mix_g, norm_ffn_g, norm_f_g, rwkv_mix, rwkv_w_rkv, rwkv_w0, rwkv_w1, rwkv_w2, rwkv_a0, rwkv_a1, rwkv_a2, rwkv_v0, rwkv_v1, rwkv_v2, rwkv_g1, rwkv_g2, rwkv_k_k, rwkv_k_a, rwkv_r_k, rwkv_lnx_w, rwkv_lnx_b, rwkv_w_o, kv_norm_g, kv_ada_w, kv_ada_b, w_kv, cmp_pos, cmp_w1, cmp_b1, cmp_w2, nsa_w_qg, nsa_w_o, moe_w_grp, moe_b_grp, moe_w_exp, moe_b_exp, moe_w_gate, moe_w_up, moe_w_down):
    P = dict(norm_mix_g=norm_mix_g, norm_ffn_g=norm_ffn_g, norm_f_g=norm_f_g,
             rwkv_mix=rwkv_mix, rwkv_w_rkv=rwkv_w_rkv, rwkv_w0=rwkv_w0, rwkv_w1=rwkv_w1, rwkv_w2=rwkv_w2,
             rwkv_a0=rwkv_a0, rwkv_a1=rwkv_a1, rwkv_a2=rwkv_a2, rwkv_v0=rwkv_v0, rwkv_v1=rwkv_v1, rwkv_v2=rwkv_v2,
             rwkv_g1=rwkv_g1, rwkv_g2=rwkv_g2, rwkv_k_k=rwkv_k_k, rwkv_k_a=rwkv_k_a,
             rwkv_r_k=rwkv_r_k.reshape(N_A, D_MODEL), rwkv_lnx_w=rwkv_lnx_w, rwkv_lnx_b=rwkv_lnx_b,
             rwkv_w_o=rwkv_w_o, kv_norm_g=kv_norm_g, w_kv=w_kv, nsa_w_qg=nsa_w_qg, nsa_w_o=nsa_w_o,
             moe_w_grp=moe_w_grp, moe_b_grp=moe_b_grp, moe_w_exp=moe_w_exp, moe_b_exp=moe_b_exp,
             moe_w_gate=moe_w_gate, moe_w_up=moe_w_up, moe_w_down=moe_w_down)
    d = D_MODEL
    hd = NSA_HEAD_DIM
    g_kv = NSA_KV_HEADS
    bp, t_p, _ = x_prompt.shape
    bs, t_s, _ = x_sample.shape

    c_all = jnp.concatenate([c_prompt, c_sample], axis=0)
    m_all = c_all.shape[0]
    c_all = _pad_to(c_all, 0, -(-m_all // SUBLANES) * SUBLANES)
    mods = ada_project(c_all, ada_w, ada_b, 1536)
    kvmod = ada_project(c_all, kv_ada_w[None], kv_ada_b[None], 1024)[0]
    cmp_consts = _compress_consts(cmp_pos, cmp_w1, cmp_b1, cmp_w2)

    def make_prompt_attend(kv):
        nch = t_p // CMP_STRIDE
        slabs = 2 * g_kv * hd // LANES
        kvc = kv.reshape(bp, t_p, 6 * g_kv * hd)[:, :, :slabs * LANES].reshape(bp, t_p * slabs, LANES)
        cmp_kv = compress_call([kvc], [pl.BlockSpec((1, t_p * slabs, LANES), lambda b: (b, 0, 0))],
                               (bp,), None, bp, nch, t_p, slabs, cmp_consts)
        kvt = kv.reshape(bp, t_p, 6, g_kv, hd).transpose(2, 0, 3, 1, 4).astype(BF16)
        return lambda qg: nsa_prompt_attend(qg, cmp_kv, kvt, bp, t_p)

    y_p, kv_p, shift_p, wkv_p = _run_trunk(
        x_prompt.reshape(bp * t_p, d), mods[:, :bp], kvmod[:bp],
        jnp.zeros((N_A, bp, d), F32), jnp.zeros((N_A, bp, RWKV_HEADS, RWKV_HEAD_DIM, RWKV_HEAD_DIM), F32),
        P, bp, t_p, make_prompt_attend, bp, min(64, t_p))

    n_pages = page_table.shape[1]
    page_rows = cache_nsa_kv.shape[1]
    cache_pages = cache_nsa_kv.reshape(cache_nsa_kv.shape[0], page_rows, 4 * g_kv * hd)
    wb = cache_win_kv.shape[1]
    cache_win = cache_win_kv.reshape(bs, wb, 2 * g_kv * hd)

    def make_sample_attend(kv):
        nch = (n_pages * page_rows + t_s) // CMP_STRIDE
        assert nch * CMP_STRIDE == n_pages * page_rows
        slabs = 4 * g_kv * hd // LANES
        pages_sl = cache_nsa_kv.reshape(cache_nsa_kv.shape[0], page_rows * slabs, LANES)
        specs = [pl.BlockSpec((1, page_rows * slabs, LANES), lambda b, pt, j=j: (pt[b, j], 0, 0))
                 for j in range(n_pages)]
        cmp_kv = compress_call([pages_sl] * n_pages, specs, (bs,), page_table, bs, nch, page_rows, slabs,
                               cmp_consts)
        return lambda qg: nsa_sample_attend(qg, cmp_kv, cache_pages, page_table, cache_win, kv, bs, t_s)

    y_s, kv_s, shift_s, wkv_s = _run_trunk(
        x_sample.reshape(bs * t_s, d), mods[:, bp:bp + bs], kvmod[bp:bp + bs],
        state_shift, state_wkv, P, bs, t_s, make_sample_attend, 4, t_s)

    kv_p5 = kv_p.reshape(bp, t_p, 6, g_kv, hd)
    kv_s5 = kv_s.reshape(bs, t_s, 6, g_kv, hd)
    win_p = kv_p5[:, -min(WINDOW, t_p):, 4:]
    win_all = jnp.concatenate([cache_win_kv, kv_s5[:, :, 4:]], axis=1)
    win_s = win_all[:, -min(WINDOW, win_all.shape[1]):]
    return (y_p.reshape(bp, t_p, d), y_s.reshape(bs, t_s, d), kv_p5[:, :, :4], kv_s5[:, :, :4],
            win_p, win_s, wkv_p, wkv_s, shift_p, shift_s)
```

```python
import functools
import math

import jax
import jax.numpy as jnp
from jax import lax
from jax.experimental import pallas as pl
from jax.experimental.pallas import tpu as pltpu

D_MODEL = 1024
DEPTH = 4
N_A = 2
RWKV_HEADS = 16
RWKV_HEAD_DIM = 64
NSA_HEADS = 16
NSA_KV_HEADS = 4
NSA_HEAD_DIM = 64
NSA_GROUP = 4
CMP_STRIDE = 16
CMP_LEN = 32
CMP_HIDDEN = 256
SEL_BLOCK = 64
SEL_TOP = 16
WINDOW = 512
N_GROUPS = 4
GROUP_EXPERTS = 4
N_EXPERTS = 16
EXPERT_FF = 512
NORM_EPS = 1e-6
LNX_EPS = 64e-5
NEG = -1e30

LANES = 128
SUBLANES = 8
VMEM_LIMIT = 56 * 1024 * 1024
BF16 = jnp.bfloat16
F32 = jnp.float32


def _cp(*sem):
    return pltpu.CompilerParams(dimension_semantics=sem, vmem_limit_bytes=VMEM_LIMIT)


def _dot(a, b):
    return jnp.dot(a, b, preferred_element_type=F32)


def _dot_nt(a, b):
    return lax.dot_general(a, b, (((1,), (1,)), ((), ())), preferred_element_type=F32)


def _split2(x):
    hi = x.astype(BF16)
    lo = (x - hi.astype(F32)).astype(BF16)
    return hi, lo


def _dot_x2(x, w_bf16):
    hi, lo = _split2(x)
    return _dot(hi, w_bf16) + _dot(lo, w_bf16)


def _dot_3pass(x, w):
    xh, xl = _split2(x)
    wh, wl = _split2(w)
    return _dot(xh, wh) + (_dot(xh, wl) + _dot(xl, wh))


def _iota(shape, dim):
    return lax.broadcasted_iota(jnp.int32, shape, dim)


def _ones_bd():
    return (_iota((LANES, LANES), 0) // 64 == _iota((LANES, LANES), 1) // 64).astype(BF16)


def _seg_sum64(x, ones_bd):
    cols = [_dot_x2(x[:, s * LANES:(s + 1) * LANES], ones_bd) for s in range(x.shape[1] // LANES)]
    return jnp.concatenate(cols, axis=1)


def _silu(x):
    return x * jax.nn.sigmoid(x)


def _norm_mod(x, g, sh, sc):
    y = x * lax.rsqrt(jnp.mean(x * x, axis=-1, keepdims=True) + NORM_EPS)
    return (y * g) * (1.0 + sc) + sh


def _mod_spec(vec, t_len, tm):
    bg, d = vec.shape
    if t_len % tm == 0:
        tpb = t_len // tm
        return vec.reshape(bg, 1, d), pl.BlockSpec((1, 1, d), lambda i, *_: (i // tpb, 0, 0))
    assert tm % t_len == 0
    n = bg * t_len
    return jnp.repeat(vec, t_len, axis=0).reshape(n // tm, tm, d), pl.BlockSpec((1, tm, d), lambda i, *_: (i, 0, 0))


def _full_spec(arr):
    nd = arr.ndim
    return pl.BlockSpec(arr.shape, lambda *_: (0,) * nd)


def _tile_rows(n, pref):
    tm = min(pref, n)
    assert n % tm == 0
    return tm


def _ada_body(c_ref, w_ref, b_ref, o_ref):
    o_ref[0] = _dot_3pass(_silu(c_ref[...]), w_ref[0]) + b_ref[0]


def ada_project(c, w, b, tn):
    nl, d, n = w.shape
    m = c.shape[0]
    return pl.pallas_call(
        _ada_body,
        out_shape=jax.ShapeDtypeStruct((nl, m, n), F32),
        grid=(nl, n // tn),
        in_specs=[pl.BlockSpec((m, d), lambda l, j: (0, 0)),
                  pl.BlockSpec((1, d, tn), lambda l, j: (l, 0, j)),
                  pl.BlockSpec((1, 1, tn), lambda l, j: (l, 0, j))],
        out_specs=pl.BlockSpec((1, m, tn), lambda l, j: (l, 0, j)),
        compiler_params=_cp("arbitrary", "arbitrary"),
        name="ada_project",
    )(c, w, b.reshape(nl, 1, n))


def _normmod_body(x_ref, g_ref, sh_ref, sc_ref, h_ref):
    h_ref[...] = _norm_mod(x_ref[...], g_ref[...], sh_ref[0], sc_ref[0])


def norm_mod(x, g, sh, sc, t_len):
    n, d = x.shape
    tm = _tile_rows(n, 512)
    sh_a, sh_s = _mod_spec(sh, t_len, tm)
    sc_a, sc_s = _mod_spec(sc, t_len, tm)
    return pl.pallas_call(
        _normmod_body,
        out_shape=jax.ShapeDtypeStruct((n, d), F32),
        grid=(n // tm,),
        in_specs=[pl.BlockSpec((tm, d), lambda i: (i, 0)), pl.BlockSpec((1, d), lambda i: (0, 0)), sh_s, sc_s],
        out_specs=pl.BlockSpec((tm, d), lambda i: (i, 0)),
        compiler_params=_cp("arbitrary"),
        name="norm_mod",
    )(x, g.reshape(1, d), sh_a, sc_a)


def _rmsnorm_body(x_ref, g_ref, o_ref):
    x = x_ref[...]
    o_ref[...] = x * lax.rsqrt(jnp.mean(x * x, axis=-1, keepdims=True) + NORM_EPS) * g_ref[...]


def rms_norm_final(x, g):
    n, d = x.shape
    tm = _tile_rows(n, 512)
    return pl.pallas_call(
        _rmsnorm_body,
        out_shape=jax.ShapeDtypeStruct((n, d), F32),
        grid=(n // tm,),
        in_specs=[pl.BlockSpec((tm, d), lambda i: (i, 0)), pl.BlockSpec((1, d), lambda i: (0, 0))],
        out_specs=pl.BlockSpec((tm, d), lambda i: (i, 0)),
        compiler_params=_cp("arbitrary"),
        name="rms_norm_final",
    )(x, g.reshape(1, d))


def _softplus(z):
    return jnp.maximum(z, 0.0) + jnp.log(1.0 + jnp.exp(-jnp.abs(z)))


def _rwkv_proj_body(has_vres, *refs):
    (h_ref, hp_ref, mix_ref, wrkv_ref, w0_ref, w1_ref, w2_ref, a0_ref, a1_ref, a2_ref,
     g1_ref, g2_ref, kk_ref, ka_ref) = refs[:14]
    refs = refs[14:]
    if has_vres:
        v0_ref, v1_ref, v2_ref, vf_ref = refs[:4]
        refs = refs[4:]
    r_out, w_out, k_out, v_out, a_out, b_out, g_out = refs

    h = h_ref[...]
    xx = hp_ref[...] - h
    mix = mix_ref[...]

    def xm(s):
        return (h + xx * mix[s:s + 1]).astype(BF16)

    r = _dot(xm(0), wrkv_ref[0])
    k = _dot(xm(1), wrkv_ref[1])
    x2 = xm(2)
    v = _dot(x2, wrkv_ref[2])
    wl = w0_ref[...] + _dot(jnp.tanh(_dot(xm(3), w1_ref[...])).astype(BF16), w2_ref[...])
    decay = jnp.exp(-jnp.exp(-_softplus(-wl) - 0.5))
    if has_vres:
        mv = jax.nn.sigmoid(v0_ref[...] + _dot(_dot(x2, v1_ref[...]).astype(BF16), v2_ref[...]))
        v = v + (vf_ref[...] - v) * mv
    a = jax.nn.sigmoid(a0_ref[...] + _dot(_dot(xm(4), a1_ref[...]).astype(BF16), a2_ref[...]))
    g = _dot(jax.nn.sigmoid(_dot(xm(5), g1_ref[...])).astype(BF16), g2_ref[...])
    kkr = k * kk_ref[...]
    ssq = _seg_sum64(kkr * kkr, _ones_bd())
    kk = kkr * lax.rsqrt(jnp.maximum(ssq, 1e-24))
    r_out[...] = r
    w_out[...] = decay
    k_out[...] = k * (1.0 + (a - 1.0) * ka_ref[...])
    v_out[...] = v
    a_out[...] = -kk
    b_out[...] = kk * a
    g_out[...] = g


def _pad_to(x, axis, size):
    pad = [(0, 0)] * x.ndim
    pad[axis] = (0, size - x.shape[axis])
    return jnp.pad(x, pad)


def rwkv_project(h, hp, P, l, v_first):
    n, d = h.shape
    tm = _tile_rows(n, 256)
    has_vres = v_first is not None
    row = lambda z: z.reshape(1, d)
    lora_in = lambda w: _pad_to(w, 1, -(-w.shape[1] // LANES) * LANES).astype(BF16)
    lora_out = lambda w: _pad_to(w, 0, -(-w.shape[0] // LANES) * LANES).astype(BF16)
    consts = [_pad_to(P['rwkv_mix'][l], 0, SUBLANES), P['rwkv_w_rkv'][l].astype(BF16),
              row(P['rwkv_w0'][l]), lora_in(P['rwkv_w1'][l]), lora_out(P['rwkv_w2'][l]),
              row(P['rwkv_a0'][l]), lora_in(P['rwkv_a1'][l]), lora_out(P['rwkv_a2'][l]),
              lora_in(P['rwkv_g1'][l]), lora_out(P['rwkv_g2'][l]),
              row(P['rwkv_k_k'][l]), row(P['rwkv_k_a'][l])]
    if has_vres:
        consts += [row(P['rwkv_v0'][l - 1]), lora_in(P['rwkv_v1'][l - 1]), lora_out(P['rwkv_v2'][l - 1])]
    tile = pl.BlockSpec((tm, d), lambda i: (i, 0))
    ins = [h, hp] + consts + ([v_first] if has_vres else [])
    in_specs = [tile, tile] + [_full_spec(c) for c in consts] + ([tile] if has_vres else [])
    return pl.pallas_call(
        functools.partial(_rwkv_proj_body, has_vres),
        out_shape=[jax.ShapeDtypeStruct((n, d), F32)] * 7,
        grid=(n // tm,),
        in_specs=in_specs,
        out_specs=[tile] * 7,
        compiler_params=_cp("arbitrary"),
        name="rwkv_project",
    )(*ins)


def _wkv_body(bb_n, tc_n, r_ref, w_ref, k_ref, v_ref, a_ref, b_ref, s0_ref, y_ref, st_ref, s_scr, y_scr):
    tc = pl.program_id(1)
    n = RWKV_HEAD_DIM
    pairs = bb_n * SUBLANES
    rows = pairs * n

    @pl.when(tc == 0)
    def _():
        for bb in range(bb_n):
            for p in range(SUBLANES):
                r0 = (bb * SUBLANES + p) * n
                s_scr[r0:r0 + n, 0:n] = s0_ref[bb, 2 * p]
                s_scr[r0:r0 + n, n:2 * n] = s0_ref[bb, 2 * p + 1]

    ones_bd = _ones_bd()
    diag = _iota((rows, LANES), 0) % n == _iota((rows, LANES), 1) % n
    grp = min(SUBLANES, tc_n)

    def read_out(yb, i):
        ysum = jnp.sum(jnp.where(diag, yb, 0.0).reshape(pairs, n, LANES), axis=1)
        for bb in range(bb_n):
            for p in range(SUBLANES):
                idx = bb * SUBLANES + p
                y_scr[bb, i:i + 1, p * LANES:(p + 1) * LANES] = ysum[idx:idx + 1]

    def group(gi, carry):
        base = pl.multiple_of(gi * grp, grp)
        tiles = [[ref[bb, pl.ds(base, grp), :] for bb in range(bb_n)]
                 for ref in (r_ref, w_ref, k_ref, v_ref, a_ref, b_ref)]

        def bcast(q, i):
            return jnp.concatenate(
                [jnp.broadcast_to(tiles[q][bb][i:i + 1, p * LANES:(p + 1) * LANES], (n, LANES))
                 for bb in range(bb_n) for p in range(SUBLANES)], axis=0)

        s = s_scr[...]
        for i in range(grp):
            parts = [(s * bcast(4, i)).astype(BF16), jnp.where(diag, bcast(3, i), 0.0).astype(BF16)]
            if i > 0:
                parts.append((s * bcast(0, i - 1)).astype(BF16))
            res = _dot(jnp.concatenate(parts, axis=0), ones_bd)
            if i > 0:
                read_out(res[2 * rows:3 * rows], i - 1)
            s = s * bcast(1, i) + res[0:rows] * bcast(5, i) + res[rows:2 * rows] * bcast(2, i)
        read_out(_dot((s * bcast(0, grp - 1)).astype(BF16), ones_bd), grp - 1)
        s_scr[...] = s
        for bb in range(bb_n):
            y_ref[bb, pl.ds(base, grp), :] = y_scr[bb]
        return carry

    lax.fori_loop(0, tc_n // grp, group, 0)

    @pl.when(tc == pl.num_programs(1) - 1)
    def _():
        for bb in range(bb_n):
            for p in range(SUBLANES):
                r0 = (bb * SUBLANES + p) * n
                st_ref[bb, 2 * p] = s_scr[r0:r0 + n, 0:n]
                st_ref[bb, 2 * p + 1] = s_scr[r0:r0 + n, n:2 * n]


def wkv_scan(r, w, k, v, a, b, s0, bsz, t_len, bb_n, tc_n):
    d = D_MODEL
    n = RWKV_HEAD_DIM
    shp = lambda z: z.reshape(bsz, t_len, d)
    tok = pl.BlockSpec((bb_n, tc_n, d), lambda i, j: (i, j, 0))
    st = pl.BlockSpec((bb_n, RWKV_HEADS, n, n), lambda i, j: (i, 0, 0, 0))
    y, s_t = pl.pallas_call(
        functools.partial(_wkv_body, bb_n, tc_n),
        out_shape=[jax.ShapeDtypeStruct((bsz, t_len, d), F32), jax.ShapeDtypeStruct(s0.shape, F32)],
        grid=(bsz // bb_n, t_len // tc_n),
        in_specs=[tok] * 6 + [st],
        out_specs=[tok, st],
        scratch_shapes=[pltpu.VMEM((bb_n * SUBLANES * n, LANES), F32),
                        pltpu.VMEM((bb_n, min(SUBLANES, tc_n), d), F32)],
        compiler_params=_cp("arbitrary", "arbitrary"),
        name="wkv_scan",
    )(shp(r), shp(w), shp(k), shp(v), shp(a), shp(b), s0)
    return y.reshape(bsz * t_len, d), s_t


def _rwkv_out_body(y_ref, r_ref, k_ref, v_ref, g_ref, x_ref, gt_ref, lw_ref, lb_ref, rk_ref, wo_ref, o_ref):
    ones_bd = _ones_bd()
    y = y_ref[...]
    inv_n = 1.0 / RWKV_HEAD_DIM
    mu = _seg_sum64(y, ones_bd) * inv_n
    yc = y - mu
    var = _seg_sum64(yc * yc, ones_bd) * inv_n
    yn = yc * lax.rsqrt(var + LNX_EPS) * lw_ref[...] + lb_ref[...]
    bonus = _seg_sum64(r_ref[...] * k_ref[...] * rk_ref[...], ones_bd) * v_ref[...]
    out = _dot(((yn + bonus) * g_ref[...]).astype(BF16), wo_ref[...])
    o_ref[...] = x_ref[...] + gt_ref[0] * out


def rwkv_output(y, r, k, v, g, x, gt, P, l, t_len):
    n, d = x.shape
    tm = _tile_rows(n, 256)
    gt_a, gt_s = _mod_spec(gt, t_len, tm)
    row = lambda z: z.reshape(1, d)
    consts = [row(P['rwkv_lnx_w'][l]), row(P['rwkv_lnx_b'][l]), row(P['rwkv_r_k'][l]), P['rwkv_w_o'][l].astype(BF16)]
    tile = pl.BlockSpec((tm, d), lambda i: (i, 0))
    return pl.pallas_call(
        _rwkv_out_body,
        out_shape=jax.ShapeDtypeStruct((n, d), F32),
        grid=(n // tm,),
        in_specs=[tile] * 6 + [gt_s] + [_full_spec(c) for c in consts],
        out_specs=tile,
        compiler_params=_cp("arbitrary"),
        name="rwkv_output",
    )(y, r, k, v, g, x, gt_a, *consts)


BIGNEG = -3.0e38


def _route(logits):
    lane = _iota(logits.shape, 1).astype(F32)
    gmask = lane < N_GROUPS
    gmax = jnp.max(jnp.where(gmask, logits, BIGNEG), axis=-1, keepdims=True)
    grp = jnp.min(jnp.where(gmask & (logits == gmax), lane, 999.0), axis=-1, keepdims=True)
    denom = jnp.sum(jnp.where(gmask, jnp.exp(jnp.where(gmask, logits - gmax, 0.0)), 0.0), axis=-1, keepdims=True)
    p_grp = 1.0 / denom
    lo = N_GROUPS + GROUP_EXPERTS * grp
    inmask = (lane >= lo) & (lane < lo + GROUP_EXPERTS)
    m1 = jnp.max(jnp.where(inmask, logits, BIGNEG), axis=-1, keepdims=True)
    i1 = jnp.min(jnp.where(inmask & (logits == m1), lane, 999.0), axis=-1, keepdims=True)
    mask2 = inmask & (lane != i1)
    m2 = jnp.max(jnp.where(mask2, logits, BIGNEG), axis=-1, keepdims=True)
    i2 = jnp.min(jnp.where(mask2 & (logits == m2), lane, 999.0), axis=-1, keepdims=True)
    e12 = jnp.exp(m2 - m1)
    w1 = p_grp / (1.0 + e12)
    w2 = p_grp * e12 / (1.0 + e12)
    return jnp.where(lane == 0, i1 - N_GROUPS,
                     jnp.where(lane == 1, i2 - N_GROUPS, jnp.where(lane == 2, w1, jnp.where(lane == 3, w2, 0.0))))


def _moe_body(x_ref, g_ref, sh_ref, sc_ref, gt_ref, wr_ref, br_ref, wg_ref, wu_ref, wd_ref, o_ref,
              h_scr, route_scr, acc_scr):
    e = pl.program_id(1)

    @pl.when(e == 0)
    def _():
        h = _norm_mod(x_ref[...], g_ref[...], sh_ref[0], sc_ref[0])
        h_scr[...] = h.astype(BF16)
        route_scr[...] = _route(_dot_3pass(h, wr_ref[...]) + br_ref[...])
        acc_scr[...] = jnp.zeros_like(acc_scr)

    h = h_scr[...]
    act = _silu(_dot(h, wg_ref[0].astype(BF16))) * _dot(h, wu_ref[0].astype(BF16))
    rt = route_scr[...]
    ef = e.astype(F32)
    coef = jnp.where(rt[:, 0:1] == ef, rt[:, 2:3], 0.0) + jnp.where(rt[:, 1:2] == ef, rt[:, 3:4], 0.0)
    acc_scr[...] += _dot((act * coef).astype(BF16), wd_ref[0].astype(BF16))

    @pl.when(e == N_EXPERTS - 1)
    def _():
        o_ref[...] = x_ref[...] + gt_ref[0] * acc_scr[...]


def moe_dense(x, sh, sc, gt, P, l, t_len):
    n, d = x.shape
    tm = _tile_rows(n, 1024)
    mods = [_mod_spec(z, t_len, tm) for z in (sh, sc, gt)]
    wr = _pad_to(jnp.concatenate([P['moe_w_grp'][l], P['moe_w_exp'][l]], axis=1), 1, LANES)
    br = _pad_to(jnp.concatenate([P['moe_b_grp'][l], P['moe_b_exp'][l]]), 0, LANES).reshape(1, LANES)
    tile = pl.BlockSpec((tm, d), lambda i, e: (i, 0))
    ff = EXPERT_FF
    return pl.pallas_call(
        _moe_body,
        out_shape=jax.ShapeDtypeStruct((n, d), F32),
        grid=(n // tm, N_EXPERTS),
        in_specs=[tile, pl.BlockSpec((1, d), lambda i, e: (0, 0))] + [m[1] for m in mods]
        + [pl.BlockSpec((d, LANES), lambda i, e: (0, 0)), pl.BlockSpec((1, LANES), lambda i, e: (0, 0)),
           pl.BlockSpec((1, d, ff), lambda i, e: (e, 0, 0)), pl.BlockSpec((1, d, ff), lambda i, e: (e, 0, 0)),
           pl.BlockSpec((1, ff, d), lambda i, e: (e, 0, 0))],
        out_specs=tile,
        scratch_shapes=[pltpu.VMEM((tm, d), BF16), pltpu.VMEM((tm, LANES), F32), pltpu.VMEM((tm, d), F32)],
        compiler_params=_cp("arbitrary", "arbitrary"),
        name="moe_dense",
    )(x, P['norm_ffn_g'][l].reshape(1, d), *[m[0] for m in mods], wr, br,
      P['moe_w_gate'][l], P['moe_w_up'][l], P['moe_w_down'][l])


def _nm_matmul_body(x_ref, g_ref, sh_ref, sc_ref, w_ref, o_ref):
    h = _norm_mod(x_ref[...], g_ref[...], sh_ref[0], sc_ref[0])
    o_ref[...] = _dot(h.astype(BF16), w_ref[...])


def norm_mod_matmul(x, g, sh, sc, w, t_len):
    n, d = x.shape
    nout = w.shape[1]
    tm = _tile_rows(n, 512)
    sh_a, sh_s = _mod_spec(sh, t_len, tm)
    sc_a, sc_s = _mod_spec(sc, t_len, tm)
    return pl.pallas_call(
        _nm_matmul_body,
        out_shape=jax.ShapeDtypeStruct((n, nout), F32),
        grid=(n // tm,),
        in_specs=[pl.BlockSpec((tm, d), lambda i: (i, 0)), pl.BlockSpec((1, d), lambda i: (0, 0)), sh_s, sc_s,
                  pl.BlockSpec((d, nout), lambda i: (0, 0))],
        out_specs=pl.BlockSpec((tm, nout), lambda i: (i, 0)),
        compiler_params=_cp("arbitrary"),
        name="norm_mod_matmul",
    )(x, g.reshape(1, d), sh_a, sc_a, w.astype(BF16))


def _kv_proj_body(x_ref, g_ref, sh_ref, sc_ref, w_ref, wt_ref, o_ref, ot_ref):
    h = _norm_mod(x_ref[...], g_ref[...], sh_ref[0], sc_ref[0]).astype(BF16)
    o_ref[...] = _dot(h, w_ref[...])
    ot_ref[0] = _dot_nt(wt_ref[...], h).astype(BF16)


def kv_project_with_transpose(x, g, sh, sc, w, n_t, bsz, t_len):
    n, d = x.shape
    nout = w.shape[1]
    tm = _tile_rows(t_len, 512)
    tpb = t_len // tm
    sh_a, sh_s = _mod_spec(sh, t_len, tm)
    sc_a, sc_s = _mod_spec(sc, t_len, tm)
    wb = w.astype(BF16)
    return pl.pallas_call(
        _kv_proj_body,
        out_shape=[jax.ShapeDtypeStruct((n, nout), F32), jax.ShapeDtypeStruct((bsz, n_t, t_len), BF16)],
        grid=(n // tm,),
        in_specs=[pl.BlockSpec((tm, d), lambda i: (i, 0)), pl.BlockSpec((1, d), lambda i: (0, 0)), sh_s, sc_s,
                  pl.BlockSpec((d, nout), lambda i: (0, 0)), pl.BlockSpec((n_t, d), lambda i: (0, 0))],
        out_specs=[pl.BlockSpec((tm, nout), lambda i: (i, 0)),
                   pl.BlockSpec((1, n_t, tm), lambda i: (i // tpb, 0, i % tpb))],
        compiler_params=_cp("arbitrary"),
        name="kv_project",
    )(x, g.reshape(1, d), sh_a, sc_a, wb, wb[:, nout - n_t:].T)


def _gelu_tanh(x):
    return 0.5 * x * (1.0 + jnp.tanh(math.sqrt(2.0 / math.pi) * (x + 0.044715 * (x * x * x))))


def _compress_body(n_blk, rows_per_blk, slabs, n_prefetch, *refs):
    refs = refs[n_prefetch:]
    row_refs = refs[:n_blk]
    pos_ref, w1_ref, b1_ref, w2_ref, o_ref, x_scr = refs[n_blk:n_blk + 6]
    hd = NSA_HEAD_DIM
    cpb = rows_per_blk // CMP_STRIDE
    nch = n_blk * cpb
    for z in range(2):
        for j in range(n_blk):
            for gp in range(NSA_KV_HEADS // 2):
                if slabs == 0:
                    tr_scr = refs[n_blk + 6]
                    c0 = (z * 2 + gp) * LANES
                    tr_scr[...] = row_refs[j][0, c0:c0 + LANES, :].T
                for s in range(CMP_STRIDE):
                    if slabs == 0:
                        src = tr_scr[pl.ds(s, cpb, stride=CMP_STRIDE), :]
                    else:
                        src = row_refs[j][0, pl.ds(s * slabs + z * 2 + gp, cpb, stride=CMP_STRIDE * slabs), :]
                    for gl in range(2):
                        g = 2 * gp + gl
                        x_scr[g * nch + j * cpb:g * nch + (j + 1) * cpb, s * hd:(s + 1) * hd] = (
                            src[:, gl * hd:(gl + 1) * hd])
        w1 = w1_ref[z]
        part = _dot(x_scr[...].astype(BF16), w1)
        pp = _dot_x2(pos_ref[z], w1)
        posb = pp[0:1, :CMP_HIDDEN] + pp[1:2, CMP_HIDDEN:] + b1_ref[z]
        w2 = w2_ref[z]
        for g in range(NSA_KV_HEADS):
            pg = part[g * nch:(g + 1) * nch]
            nxt = pltpu.roll(pg[:, CMP_HIDDEN:], nch - 1, 0)
            hid = posb + pg[:, :CMP_HIDDEN] + nxt
            o_ref[z, 0, g] = _dot(_gelu_tanh(hid).astype(BF16), w2)


def _compress_consts(cmp_pos, cmp_w1, cmp_b1, cmp_w2):
    hd = NSA_HEAD_DIM
    k_half = CMP_STRIDE * hd
    w1 = cmp_w1.reshape(2, 2, k_half, CMP_HIDDEN)
    w1cat = jnp.concatenate([w1[:, 0], w1[:, 1]], axis=-1).astype(BF16)
    pos = _pad_to(cmp_pos.reshape(2, 2, k_half), 1, SUBLANES)
    return pos, w1cat, cmp_b1.reshape(2, 1, CMP_HIDDEN), cmp_w2.astype(BF16)


def compress_call(row_arrays, row_specs, grid, prefetch, bsz, nch, rows_per_blk, slabs, consts):
    n_blk = len(row_arrays)
    np_ = 1 if prefetch is not None else 0
    cspecs = [pl.BlockSpec(c.shape, (lambda *_, nd=c.ndim: (0,) * nd)) for c in consts]
    gs = pltpu.PrefetchScalarGridSpec(
        num_scalar_prefetch=np_, grid=grid, in_specs=list(row_specs) + cspecs,
        out_specs=pl.BlockSpec((2, 1, NSA_KV_HEADS, nch, NSA_HEAD_DIM), lambda b, *_: (0, b, 0, 0, 0)),
        scratch_shapes=[pltpu.VMEM((NSA_KV_HEADS * nch, CMP_STRIDE * NSA_HEAD_DIM), F32)]
        + ([pltpu.VMEM((rows_per_blk, LANES), F32)] if slabs == 0 else []))
    args = ([prefetch] if np_ else []) + list(row_arrays) + list(consts)
    return pl.pallas_call(
        functools.partial(_compress_body, n_blk, rows_per_blk, slabs, np_),
        out_shape=jax.ShapeDtypeStruct((2, bsz, NSA_KV_HEADS, nch, NSA_HEAD_DIM), F32),
        grid_spec=gs, compiler_params=_cp("arbitrary"), name="nsa_compress",
    )(*args)


def _masked_softmax(s, mask):
    s = jnp.where(mask, s, NEG)
    e = jnp.exp(s - jnp.max(s, axis=-1, keepdims=True))
    return jnp.where(mask, e / jnp.sum(e, axis=-1, keepdims=True), 0.0)


def _overlap(nch, nbp):
    n = _iota((nch, nbp), 0) * CMP_STRIDE
    j = _iota((nch, nbp), 1) * SEL_BLOCK
    return ((n <= j + SEL_BLOCK - 1) & (n + CMP_LEN - 1 >= j)).astype(BF16)


def _compressed_branch(qs, kc, vc, slope, trow, nc_valid, n_tok):
    nch = kc.shape[0]
    ncol = _iota((1, nch), 1)
    d_c = trow - (ncol * CMP_STRIDE + (CMP_LEN - 1))
    m_c = (d_c >= 0) & (ncol < nc_valid)
    s_c = _dot_nt(qs, kc.astype(BF16)) - slope * d_c.astype(F32)
    p_c = _masked_softmax(s_c, m_c)
    o_c = _dot(p_c.astype(BF16), vc.astype(BF16))
    psum = p_c[0:n_tok]
    for p in range(1, NSA_GROUP):
        psum = psum + p_c[p * n_tok:(p + 1) * n_tok]
    return o_c, psum


def _select_blocks(imp, tpos, nb):
    nbp = imp.shape[1]
    blk = _iota((1, nbp), 1)
    cur = tpos // SEL_BLOCK
    forced = (blk == 0) | (blk == cur) | (blk == cur - 1)
    score = jnp.where(blk * SEL_BLOCK <= tpos, jnp.where(forced, -NEG, imp), NEG)
    score = jnp.where(blk < nb, score, 2.0 * NEG)
    rank = jnp.zeros(imp.shape, F32)
    blk_f = blk.astype(F32)
    for i in range(nb):
        ci = score[:, i:i + 1]
        tie = jnp.where(blk_f > float(i), 1.0, 0.0)
        rank = rank + jnp.where(ci > score, 1.0, jnp.where(ci == score, tie, 0.0))
    return jnp.where(rank < float(min(SEL_TOP, nb)), 1.0, 0.0)


def _expand_sel(sel, k0, tk):
    nbp = sel.shape[1]
    e_mat = (_iota((nbp, tk), 0) == (k0 + _iota((nbp, tk), 1)) // SEL_BLOCK).astype(BF16)
    m = _dot(sel.astype(BF16), e_mat)
    return jnp.concatenate([m] * NSA_GROUP, axis=0)


def _gate_combine(gate, branches, n_tok):
    outs = []
    for p in range(NSA_GROUP):
        acc = None
        for c, o in enumerate(branches):
            term = gate[:, p * 3 + c:p * 3 + c + 1] * o[p * n_tok:(p + 1) * n_tok]
            acc = term if acc is None else acc + term
        outs.append(acc)
    return jnp.concatenate(outs, axis=1)


def _stack_heads(qg, scale):
    hd = NSA_HEAD_DIM
    return jnp.concatenate([qg[:, p * hd:(p + 1) * hd] for p in range(NSA_GROUP)], axis=0) * scale


AUG_BLK = 64
AUG_ROWS = 128


def _key_aug(t_len):
    pos = jnp.arange(t_len, dtype=jnp.int32)[None, :]
    r = jnp.arange(AUG_ROWS, dtype=jnp.int32)[:, None]
    blk = pos // SEL_BLOCK
    onehot = (r == blk) & (r < AUG_BLK)
    hi = (r >= AUG_BLK) & (r < AUG_BLK + 3)
    lo = (r >= AUG_BLK + 3) & (r < AUG_BLK + 6)
    val = jnp.where(onehot, 1, jnp.where(hi, blk * SEL_BLOCK, jnp.where(lo, pos % SEL_BLOCK, 0)))
    return val.astype(BF16)


def _slope_aug(n_tok):
    sl = _alibi_slope_rows(n_tok)
    hi = sl.astype(BF16).astype(F32)
    mid = (sl - hi).astype(BF16).astype(F32)
    lo = (sl - hi - mid).astype(BF16).astype(F32)
    return _pad_to(jnp.concatenate([hi, mid, lo, hi, mid, lo], axis=-1), 2, AUG_BLK)


def _nsa_prompt_body(tq, tk, t_len, q_ref, gate_ref, slope_ref, saug_ref, kaug_ref, kc_ref, vc_ref,
                     ks_ref, vs_ref, kw_ref, vw_ref, o_ref):
    qi = pl.program_id(2)
    q0 = qi * tq
    rows = NSA_GROUP * tq
    nch = t_len // CMP_STRIDE
    nb = t_len // SEL_BLOCK
    assert nb <= AUG_BLK and tk % tq == 0
    qs = _stack_heads(q_ref[0], NSA_HEAD_DIM ** -0.5).astype(BF16)
    slope = slope_ref[0]
    trow = q0 + _iota((rows, 1), 0) % tq
    tpos = q0 + _iota((tq, 1), 0)

    o_c, psum = _compressed_branch(qs, kc_ref[0, 0, 0], vc_ref[0, 0, 0], slope, trow, nch - 1, tq)
    imp = _dot_x2(psum, _overlap(nch, AUG_BLK))
    sel = _select_blocks(imp, tpos, nb)
    selbias = jnp.concatenate([jnp.where(sel > 0.5, 0.0, NEG)] * NSA_GROUP, axis=0)
    saug = saug_ref[0]
    q_sel = jnp.concatenate([selbias, saug], axis=1).astype(BF16)
    q_win = jnp.concatenate([jnp.zeros_like(selbias), saug], axis=1).astype(BF16)

    def scores(k_ref, q_aug, k0):
        return _dot(qs, k_ref[0, :, pl.ds(k0, tk)]) + _dot(q_aug, kaug_ref[:, pl.ds(k0, tk)])

    def online(carry, s, v_ref, k0):
        m, l, acc = carry
        m_new = jnp.maximum(m, jnp.max(s, axis=-1, keepdims=True))
        alpha = jnp.exp(m - m_new)
        p = jnp.exp(s - m_new)
        l = alpha * l + jnp.sum(p, axis=-1, keepdims=True)
        acc = alpha * acc + _dot_nt(p.astype(BF16), v_ref[0, :, pl.ds(k0, tk)])
        return m_new, l, acc

    def dist(k0):
        return trow - (k0 + _iota((1, tk), 1))

    def finish(carry):
        _, l, acc = carry
        return acc * jnp.where(l > 0.0, 1.0 / l, 0.0)

    init = (jnp.full((rows, 1), NEG, F32), jnp.zeros((rows, 1), F32), jnp.zeros((rows, NSA_HEAD_DIM), F32))
    kt_last = q0 // tk
    k_last = pl.multiple_of(kt_last * tk, tk)

    def sel_body(kt, carry):
        k0 = pl.multiple_of(kt * tk, tk)
        return online(carry, scores(ks_ref, q_sel, k0), vs_ref, k0)

    carry = lax.fori_loop(0, kt_last, sel_body, init)
    s = jnp.where(dist(k_last) >= 0, scores(ks_ref, q_sel, k_last), NEG)
    o_s = finish(online(carry, s, vs_ref, k_last))

    d = dist(k_last)
    s = jnp.where((d >= 0) & (d < WINDOW), scores(kw_ref, q_win, k_last), NEG)
    carry = online(init, s, vw_ref, k_last)
    for j in range(1, -(-(WINDOW - 1) // tk) + 1):
        kp = kt_last - j
        k0 = pl.multiple_of(jnp.maximum(kp, 0) * tk, tk)
        d = dist(k0)
        s = jnp.where((d >= 0) & (d < WINDOW) & (kp >= 0), scores(kw_ref, q_win, k0), NEG)
        carry = online(carry, s, vw_ref, k0)
    o_w = finish(carry)

    gate = jax.nn.sigmoid(gate_ref[0, 0])
    o_ref[0] = _gate_combine(gate, (o_c, o_s, o_w), tq)


def _alibi_slope_rows(n_tok):
    h = jnp.arange(1, NSA_HEADS + 1, dtype=F32)
    sl = jnp.exp2(-8.0 * h / NSA_HEADS).reshape(NSA_KV_HEADS, NSA_GROUP, 1)
    return jnp.broadcast_to(sl, (NSA_KV_HEADS, NSA_GROUP, n_tok)).reshape(NSA_KV_HEADS, NSA_GROUP * n_tok, 1)


def _gate_array(qg, bsz, t_len):
    gates = qg[:, NSA_HEADS * NSA_HEAD_DIM:NSA_HEADS * NSA_HEAD_DIM + 3 * NSA_HEADS]
    gates = gates.reshape(bsz, t_len, NSA_KV_HEADS, 3 * NSA_GROUP).transpose(0, 2, 1, 3)
    return _pad_to(gates, 3, LANES)


def nsa_prompt_attend(qg, cmp_kv, kvt, bsz, t_len):
    tq = min(128, t_len)
    tk = min(512, t_len)
    gq = NSA_GROUP * NSA_HEAD_DIM
    hd = NSA_HEAD_DIM
    nch = cmp_kv.shape[3]
    q3 = qg.reshape(bsz, t_len, qg.shape[1])
    full_t = lambda z: pl.BlockSpec((1, hd, t_len), lambda b, g, i, z=z: (b, z * NSA_KV_HEADS + g, 0))
    cmp_spec = lambda z: pl.BlockSpec((1, 1, 1, nch, hd), lambda b, g, i, z=z: (z, b, g, 0, 0))
    rows = NSA_GROUP * tq

    out = pl.pallas_call(
        functools.partial(_nsa_prompt_body, tq, tk, t_len),
        out_shape=jax.ShapeDtypeStruct((bsz, t_len, NSA_HEADS * hd), F32),
        grid=(bsz, NSA_KV_HEADS, t_len // tq),
        in_specs=[pl.BlockSpec((1, tq, gq), lambda b, g, i: (b, i, g)),
                  pl.BlockSpec((1, 1, tq, LANES), lambda b, g, i: (b, g, i, 0)),
                  pl.BlockSpec((1, rows, 1), lambda b, g, i: (g, 0, 0)),
                  pl.BlockSpec((1, rows, AUG_BLK), lambda b, g, i: (g, 0, 0)),
                  pl.BlockSpec((AUG_ROWS, t_len), lambda b, g, i: (0, 0)),
                  cmp_spec(0), cmp_spec(1), full_t(0), full_t(1), full_t(2), full_t(3)],
        out_specs=pl.BlockSpec((1, tq, gq), lambda b, g, i: (b, i, g)),
        compiler_params=_cp("arbitrary", "arbitrary", "arbitrary"),
        name="nsa_prompt_attend",
    )(q3, _gate_array(qg, bsz, t_len), _alibi_slope_rows(tq), _slope_aug(tq), _key_aug(t_len),
      cmp_kv, cmp_kv, kvt, kvt, kvt, kvt)
    return out.reshape(bsz * t_len, NSA_HEADS * hd)


def _nsa_sample_body(n_pages, page_rows, s_len, wb, pt_ref, q_ref, gate_ref, slope_ref, kc_ref, vc_ref, *refs):
    page_refs = refs[:n_pages]
    new_ref, win_ref, o_ref = refs[n_pages:]
    del pt_ref
    hd = NSA_HEAD_DIM
    gq = NSA_GROUP * hd
    past_len = n_pages * page_rows
    rows = NSA_GROUP * s_len
    nch = kc_ref.shape[3]
    nb = -(-(past_len + s_len) // SEL_BLOCK)
    nbp = 64
    assert nb <= nbp and page_rows % SEL_BLOCK == 0
    trow = past_len + _iota((rows, 1), 0) % s_len
    tpos = past_len + _iota((s_len, 1), 0)
    tloc = _iota((rows, 1), 0) % s_len
    q_all = q_ref[0]
    new = new_ref[0]
    outs = []
    for g in range(NSA_KV_HEADS):
        qs = _stack_heads(q_all[:, g * gq:(g + 1) * gq], hd ** -0.5)
        qsb = qs.astype(BF16)
        slope = slope_ref[g]
        o_c, psum = _compressed_branch(qsb, kc_ref[0, 0, g], vc_ref[0, 0, g], slope, trow, nch - 1, s_len)
        imp = _dot_x2(psum, _overlap(nch, nbp))
        sel = _select_blocks(imp, tpos, nb)

        def new_scores(z):
            knew = new[:, z * gq + g * hd:z * gq + (g + 1) * hd]
            cols = []
            for t2 in range(s_len):
                sc = jnp.sum(qs * knew[t2:t2 + 1], axis=-1, keepdims=True)
                cols.append((sc - slope * (tloc - t2).astype(F32), tloc >= t2))
            return cols

        def finish(tiles, cols, vals, vnew):
            m = jnp.full((rows, 1), NEG, F32)
            for s, mk in tiles + cols:
                m = jnp.maximum(m, jnp.max(jnp.where(mk, s, NEG), axis=-1, keepdims=True))
            l = jnp.zeros((rows, 1), F32)
            acc = jnp.zeros((rows, hd), F32)
            for (s, mk), v in zip(tiles, vals):
                p = jnp.where(mk, jnp.exp(jnp.where(mk, s, NEG) - m), 0.0)
                l = l + jnp.sum(p, axis=-1, keepdims=True)
                acc = acc + _dot_nt(p.astype(BF16), v.astype(BF16))
            for t2, (s, mk) in enumerate(cols):
                p = jnp.where(mk, jnp.exp(jnp.where(mk, s, NEG) - m), 0.0)
                l = l + p
                acc = acc + p * vnew[t2:t2 + 1]
            return acc * jnp.where(l > 0.0, 1.0 / l, 0.0)

        tiles, vals = [], []
        for j in range(n_pages):
            page = page_refs[j]
            k0 = j * page_rows
            s = _dot(qsb, page[0, g * hd:(g + 1) * hd, :].astype(BF16))
            d = trow - (k0 + _iota((1, page_rows), 1))
            s = s - slope * d.astype(F32)
            tiles.append((s, (d >= 0) & (_expand_sel(sel, k0, page_rows) > 0.5)))
            vals.append(page[0, gq + g * hd:gq + (g + 1) * hd, :])
        cols = new_scores(2)
        blk_new = past_len // SEL_BLOCK
        sel_new = jnp.concatenate([sel[:, blk_new:blk_new + 1]] * NSA_GROUP, axis=0) > 0.5
        cols = [(s, mk & sel_new) for s, mk in cols]
        o_s = finish(tiles, cols, vals, new[:, 3 * gq + g * hd:3 * gq + (g + 1) * hd])

        s = _dot(qsb, win_ref[0, g * hd:(g + 1) * hd, :].astype(BF16))
        d = trow - (past_len - wb + _iota((1, wb), 1))
        s = s - slope * d.astype(F32)
        cols = [(sc, mk) for sc, mk in new_scores(4)]
        o_w = finish([(s, (d >= 0) & (d < WINDOW))], cols, [win_ref[0, gq + g * hd:gq + (g + 1) * hd, :]],
                     new[:, 5 * gq + g * hd:5 * gq + (g + 1) * hd])
        outs.append(_gate_combine(jax.nn.sigmoid(gate_ref[0, g]), (o_c, o_s, o_w), s_len))
    o_ref[0] = jnp.concatenate(outs, axis=1)


def nsa_sample_attend(qg, cmp_kv, cache_pages, page_table, cache_win, kv_new, bsz, s_len):
    n_pages = page_table.shape[1]
    page_rows = cache_pages.shape[2]
    wb = cache_win.shape[2]
    gq = NSA_GROUP * NSA_HEAD_DIM
    half = 2 * NSA_KV_HEADS * NSA_HEAD_DIM
    nch = cmp_kv.shape[3]
    q3 = qg.reshape(bsz, s_len, qg.shape[1])
    cmp_spec = lambda z: pl.BlockSpec((1, 1, NSA_KV_HEADS, nch, NSA_HEAD_DIM), lambda b, pt, z=z: (z, b, 0, 0, 0))
    page_specs = [pl.BlockSpec((1, half, page_rows), lambda b, pt, j=j: (pt[b, j], 1, 0)) for j in range(n_pages)]
    gs = pltpu.PrefetchScalarGridSpec(
        num_scalar_prefetch=1, grid=(bsz,),
        in_specs=[pl.BlockSpec((1, s_len, q3.shape[2]), lambda b, pt: (b, 0, 0)),
                  pl.BlockSpec((1, NSA_KV_HEADS, s_len, LANES), lambda b, pt: (b, 0, 0, 0)),
                  pl.BlockSpec((NSA_KV_HEADS, NSA_GROUP * s_len, 1), lambda b, pt: (0, 0, 0)),
                  cmp_spec(0), cmp_spec(1)] + page_specs
        + [pl.BlockSpec((1, s_len, kv_new.shape[1]), lambda b, pt: (b, 0, 0)),
           pl.BlockSpec((1, half, wb), lambda b, pt: (b, 0, 0))],
        out_specs=pl.BlockSpec((1, s_len, NSA_HEADS * NSA_HEAD_DIM), lambda b, pt: (b, 0, 0)))
    out = pl.pallas_call(
        functools.partial(_nsa_sample_body, n_pages, page_rows, s_len, wb),
        out_shape=jax.ShapeDtypeStruct((bsz, s_len, NSA_HEADS * NSA_HEAD_DIM), F32),
        grid_spec=gs, compiler_params=_cp("arbitrary"), name="nsa_sample_attend",
    )(page_table, q3, _gate_array(qg, bsz, s_len), _alibi_slope_rows(s_len), cmp_kv, cmp_kv,
      *([cache_pages] * n_pages), kv_new.reshape(bsz, s_len, kv_new.shape[1]), cache_win)
    return out.reshape(bsz * s_len, NSA_HEADS * NSA_HEAD_DIM)


def _resid_matmul_body(o_ref, x_ref, gt_ref, w_ref, out_ref):
    out_ref[...] = x_ref[...] + gt_ref[0] * _dot(o_ref[...].astype(BF16), w_ref[...])


def resid_matmul(o, x, gt, w, t_len):
    n, d = x.shape
    tm = _tile_rows(n, 512)
    gt_a, gt_s = _mod_spec(gt, t_len, tm)
    tile = pl.BlockSpec((tm, d), lambda i: (i, 0))
    return pl.pallas_call(
        _resid_matmul_body,
        out_shape=jax.ShapeDtypeStruct((n, d), F32),
        grid=(n // tm,),
        in_specs=[pl.BlockSpec((tm, o.shape[1]), lambda i: (i, 0)), tile, gt_s, _full_spec(w)],
        out_specs=tile,
        compiler_params=_cp("arbitrary"),
        name="resid_matmul",
    )(o, x, gt_a, w.astype(BF16))


def _run_trunk(x, mods, kvmod, shift0, wkv0, P, bsz, t_len, make_attend, scan_bb, scan_tc):
    d = D_MODEL
    v_first, attend, kv = None, None, None
    shifts, states = [], []
    for l in range(DEPTH):
        sh1, sc1, gt1, sh2, sc2, gt2 = [mods[l][:, i * d:(i + 1) * d] for i in range(6)]
        if l < N_A:
            h = norm_mod(x, P['norm_mix_g'][l], sh1, sc1, t_len)
            h3 = h.reshape(bsz, t_len, d)
            hp = jnp.concatenate([shift0[l][:, None, :], h3[:, :-1]], axis=1).reshape(bsz * t_len, d)
            r, w, k, v, a, b, g = rwkv_project(h, hp, P, l, v_first)
            if v_first is None:
                v_first = v
            y, s_t = wkv_scan(r, w, k, v, a, b, wkv0[l], bsz, t_len, scan_bb, scan_tc)
            x = rwkv_output(y, r, k, v, g, x, gt1, P, l, t_len)
            shifts.append(h3[:, -1])
            states.append(s_t)
        else:
            w_qg = _pad_to(P['nsa_w_qg'][l - N_A], 1, 9 * LANES)
            qg = norm_mod_matmul(x, P['norm_mix_g'][l], sh1, sc1, w_qg, t_len)
            x = resid_matmul(attend(qg), x, gt1, P['nsa_w_o'][l - N_A], t_len)
        x = moe_dense(x, sh2, sc2, gt2, P, l, t_len)
        if l == N_A - 1:
            kv, attend = make_attend(x, kvmod[:, :d], kvmod[:, d:])
    return rms_norm_final(x, P['norm_f_g']), kv, jnp.stack(shifts), jnp.stack(states)


def kernel(x_prompt, x_sample, cache_nsa_kv, cache_win_kv, state_wkv, state_shift, page_table, c_prompt, c_sample, ada_w, ada_b, norm_mix_g, norm_ffn_g, norm_f_g, rwkv_mix, rwkv_w_rkv, rwkv_w0, rwkv_w1, rwkv_w2, rwkv_a0, rwkv_a1, rwkv_a2, rwkv_v0, rwkv_v1, rwkv_v2, rwkv_g1, rwkv_g2, rwkv_k_k, rwkv_k_a, rwkv_r_k, rwkv_lnx_w, rwkv_lnx_b, rwkv_w_o, kv_norm_g, kv_ada_w, kv_ada_b, w_kv, cmp_pos, cmp_w1, cmp_b1, cmp_w2, nsa_w_qg, nsa_w_o, moe_w_grp, moe_b_grp, moe_w_exp, moe_b_exp, moe_w_gate, moe_w_up, moe_w_down):
    P = dict(norm_mix_g=norm_mix_g, norm_ffn_g=norm_ffn_g, norm_f_g=norm_f_g,
             rwkv_mix=rwkv_mix, rwkv_w_rkv=rwkv_w_rkv, rwkv_w0=rwkv_w0, rwkv_w1=rwkv_w1, rwkv_w2=rwkv_w2,
             rwkv_a0=rwkv_a0, rwkv_a1=rwkv_a1, rwkv_a2=rwkv_a2, rwkv_v0=rwkv_v0, rwkv_v1=rwkv_v1, rwkv_v2=rwkv_v2,
             rwkv_g1=rwkv_g1, rwkv_g2=rwkv_g2, rwkv_k_k=rwkv_k_k, rwkv_k_a=rwkv_k_a,
             rwkv_r_k=rwkv_r_k.reshape(N_A, D_MODEL), rwkv_lnx_w=rwkv_lnx_w, rwkv_lnx_b=rwkv_lnx_b,
             rwkv_w_o=rwkv_w_o, kv_norm_g=kv_norm_g, w_kv=w_kv, nsa_w_qg=nsa_w_qg, nsa_w_o=nsa_w_o,
             moe_w_grp=moe_w_grp, moe_b_grp=moe_b_grp, moe_w_exp=moe_w_exp, moe_b_exp=moe_b_exp,
             moe_w_gate=moe_w_gate, moe_w_up=moe_w_up, moe_w_down=moe_w_down)
    d = D_MODEL
    hd = NSA_HEAD_DIM
    g_kv = NSA_KV_HEADS
    bp, t_p, _ = x_prompt.shape
    bs, t_s, _ = x_sample.shape

    c_all = jnp.concatenate([c_prompt, c_sample], axis=0)
    m_all = c_all.shape[0]
    c_all = _pad_to(c_all, 0, -(-m_all // SUBLANES) * SUBLANES)
    mods = ada_project(c_all, ada_w, ada_b, 1536)
    kvmod = ada_project(c_all, kv_ada_w[None], kv_ada_b[None], 1024)[0]
    cmp_consts = _compress_consts(cmp_pos, cmp_w1, cmp_b1, cmp_w2)

    def make_prompt_attend(x, kv_sh, kv_sc):
        nch = t_p // CMP_STRIDE
        kv, kvt = kv_project_with_transpose(x, kv_norm_g, kv_sh, kv_sc, w_kv, 4 * g_kv * hd, bp, t_p)
        slabs = 2 * g_kv * hd // LANES
        kvc = kv.reshape(bp, t_p, 6 * g_kv * hd)[:, :, :slabs * LANES].reshape(bp, t_p * slabs, LANES)
        cmp_kv = compress_call([kvc], [pl.BlockSpec((1, t_p * slabs, LANES), lambda b: (b, 0, 0))],
                               (bp,), None, bp, nch, t_p, slabs, cmp_consts)
        return kv, lambda qg: nsa_prompt_attend(qg, cmp_kv, kvt, bp, t_p)

    y_p, kv_p, shift_p, wkv_p = _run_trunk(
        x_prompt.reshape(bp * t_p, d), mods[:, :bp], kvmod[:bp],
        jnp.zeros((N_A, bp, d), F32), jnp.zeros((N_A, bp, RWKV_HEADS, RWKV_HEAD_DIM, RWKV_HEAD_DIM), F32),
        P, bp, t_p, make_prompt_attend, bp, min(64, t_p))

    n_pages = page_table.shape[1]
    page_rows = cache_nsa_kv.shape[1]
    cache_pages = jnp.transpose(cache_nsa_kv, (0, 2, 3, 4, 1)).reshape(cache_nsa_kv.shape[0], 4 * g_kv * hd, page_rows)
    wb = cache_win_kv.shape[1]
    cache_win = jnp.transpose(cache_win_kv, (0, 2, 3, 4, 1)).reshape(bs, 2 * g_kv * hd, wb)

    def make_sample_attend(x, kv_sh, kv_sc):
        nch = (n_pages * page_rows + t_s) // CMP_STRIDE
        assert nch * CMP_STRIDE == n_pages * page_rows
        kv = norm_mod_matmul(x, kv_norm_g, kv_sh, kv_sc, w_kv, t_s)
        specs = [pl.BlockSpec((1, 2 * g_kv * hd, page_rows), lambda b, pt, j=j: (pt[b, j], 0, 0))
                 for j in range(n_pages)]
        cmp_kv = compress_call([cache_pages] * n_pages, specs, (bs,), page_table, bs, nch, page_rows, 0,
                               cmp_consts)
        return kv, lambda qg: nsa_sample_attend(qg, cmp_kv, cache_pages, page_table, cache_win, kv, bs, t_s)

    y_s, kv_s, shift_s, wkv_s = _run_trunk(
        x_sample.reshape(bs * t_s, d), mods[:, bp:bp + bs], kvmod[bp:bp + bs],
        state_shift, state_wkv, P, bs, t_s, make_sample_attend, 4, t_s)

    kv_p5 = kv_p.reshape(bp, t_p, 6, g_kv, hd)
    kv_s5 = kv_s.reshape(bs, t_s, 6, g_kv, hd)
    win_p = kv_p5[:, -min(WINDOW, t_p):, 4:]
    win_all = jnp.concatenate([cache_win_kv, kv_s5[:, :, 4:]], axis=1)
    win_s = win_all[:, -min(WINDOW, win_all.shape[1]):]
    return (y_p.reshape(bp, t_p, d), y_s.reshape(bs, t_s, d), kv_p5[:, :, :4], kv_s5[:, :, :4],
            win_p, win_s, wkv_p, wkv_s, shift_p, shift_s)
```

```python
import functools
import math

import jax
import jax.numpy as jnp
from jax import lax
from jax.experimental import pallas as pl
from jax.experimental.pallas import tpu as pltpu

D_MODEL = 1024
DEPTH = 4
N_A = 2
RWKV_HEADS = 16
RWKV_HEAD_DIM = 64
NSA_HEADS = 16
NSA_KV_HEADS = 4
NSA_HEAD_DIM = 64
NSA_GROUP = 4
CMP_STRIDE = 16
CMP_LEN = 32
CMP_HIDDEN = 256
SEL_BLOCK = 64
SEL_TOP = 16
WINDOW = 512
N_GROUPS = 4
GROUP_EXPERTS = 4
N_EXPERTS = 16
EXPERT_FF = 512
NORM_EPS = 1e-6
LNX_EPS = 64e-5
NEG = -1e30

LANES = 128
SUBLANES = 8
VMEM_LIMIT = 56 * 1024 * 1024
BF16 = jnp.bfloat16
F32 = jnp.float32


def _cp(*sem):
    return pltpu.CompilerParams(dimension_semantics=sem, vmem_limit_bytes=VMEM_LIMIT)


def _dot(a, b):
    return jnp.dot(a, b, preferred_element_type=F32)


def _dot_nt(a, b):
    return lax.dot_general(a, b, (((1,), (1,)), ((), ())), preferred_element_type=F32)


def _split2(x):
    hi = x.astype(BF16)
    lo = (x - hi.astype(F32)).astype(BF16)
    return hi, lo


def _dot_x2(x, w_bf16):
    hi, lo = _split2(x)
    return _dot(hi, w_bf16) + _dot(lo, w_bf16)


def _dot_3pass(x, w):
    xh, xl = _split2(x)
    wh, wl = _split2(w)
    return _dot(xh, wh) + (_dot(xh, wl) + _dot(xl, wh))


def _iota(shape, dim):
    return lax.broadcasted_iota(jnp.int32, shape, dim)


def _ones_bd():
    return (_iota((LANES, LANES), 0) // 64 == _iota((LANES, LANES), 1) // 64).astype(BF16)


def _seg_sum64(x, ones_bd):
    cols = [_dot_x2(x[:, s * LANES:(s + 1) * LANES], ones_bd) for s in range(x.shape[1] // LANES)]
    return jnp.concatenate(cols, axis=1)


def _silu(x):
    return x * jax.nn.sigmoid(x)


def _norm_mod(x, g, sh, sc):
    y = x * lax.rsqrt(jnp.mean(x * x, axis=-1, keepdims=True) + NORM_EPS)
    return (y * g) * (1.0 + sc) + sh


def _mod_spec(vec, t_len, tm):
    bg, d = vec.shape
    if t_len % tm == 0:
        tpb = t_len // tm
        return vec.reshape(bg, 1, d), pl.BlockSpec((1, 1, d), lambda i, *_: (i // tpb, 0, 0))
    assert tm % t_len == 0
    n = bg * t_len
    return jnp.repeat(vec, t_len, axis=0).reshape(n // tm, tm, d), pl.BlockSpec((1, tm, d), lambda i, *_: (i, 0, 0))


def _full_spec(arr):
    nd = arr.ndim
    return pl.BlockSpec(arr.shape, lambda *_: (0,) * nd)


def _tile_rows(n, pref):
    tm = min(pref, n)
    assert n % tm == 0
    return tm


def _ada_body(c_ref, w_ref, b_ref, o_ref):
    o_ref[0] = _dot_3pass(_silu(c_ref[...]), w_ref[0]) + b_ref[0]


def ada_project(c, w, b, tn):
    nl, d, n = w.shape
    m = c.shape[0]
    return pl.pallas_call(
        _ada_body,
        out_shape=jax.ShapeDtypeStruct((nl, m, n), F32),
        grid=(nl, n // tn),
        in_specs=[pl.BlockSpec((m, d), lambda l, j: (0, 0)),
                  pl.BlockSpec((1, d, tn), lambda l, j: (l, 0, j)),
                  pl.BlockSpec((1, 1, tn), lambda l, j: (l, 0, j))],
        out_specs=pl.BlockSpec((1, m, tn), lambda l, j: (l, 0, j)),
        compiler_params=_cp("arbitrary", "arbitrary"),
        name="ada_project",
    )(c, w, b.reshape(nl, 1, n))


def _normmod_body(x_ref, g_ref, sh_ref, sc_ref, h_ref):
    h_ref[...] = _norm_mod(x_ref[...], g_ref[...], sh_ref[0], sc_ref[0])


def norm_mod(x, g, sh, sc, t_len):
    n, d = x.shape
    tm = _tile_rows(n, 512)
    sh_a, sh_s = _mod_spec(sh, t_len, tm)
    sc_a, sc_s = _mod_spec(sc, t_len, tm)
    return pl.pallas_call(
        _normmod_body,
        out_shape=jax.ShapeDtypeStruct((n, d), F32),
        grid=(n // tm,),
        in_specs=[pl.BlockSpec((tm, d), lambda i: (i, 0)), pl.BlockSpec((1, d), lambda i: (0, 0)), sh_s, sc_s],
        out_specs=pl.BlockSpec((tm, d), lambda i: (i, 0)),
        compiler_params=_cp("arbitrary"),
        name="norm_mod",
    )(x, g.reshape(1, d), sh_a, sc_a)


def _rmsnorm_body(x_ref, g_ref, o_ref):
    x = x_ref[...]
    o_ref[...] = x * lax.rsqrt(jnp.mean(x * x, axis=-1, keepdims=True) + NORM_EPS) * g_ref[...]


def rms_norm_final(x, g):
    n, d = x.shape
    tm = _tile_rows(n, 512)
    return pl.pallas_call(
        _rmsnorm_body,
        out_shape=jax.ShapeDtypeStruct((n, d), F32),
        grid=(n // tm,),
        in_specs=[pl.BlockSpec((tm, d), lambda i: (i, 0)), pl.BlockSpec((1, d), lambda i: (0, 0))],
        out_specs=pl.BlockSpec((tm, d), lambda i: (i, 0)),
        compiler_params=_cp("arbitrary"),
        name="rms_norm_final",
    )(x, g.reshape(1, d))


def _softplus(z):
    return jnp.maximum(z, 0.0) + jnp.log(1.0 + jnp.exp(-jnp.abs(z)))


def _rwkv_proj_body(has_vres, *refs):
    (h_ref, hp_ref, mix_ref, wrkv_ref, w0_ref, w1_ref, w2_ref, a0_ref, a1_ref, a2_ref,
     g1_ref, g2_ref, kk_ref, ka_ref) = refs[:14]
    refs = refs[14:]
    if has_vres:
        v0_ref, v1_ref, v2_ref, vf_ref = refs[:4]
        refs = refs[4:]
    r_out, w_out, k_out, v_out, a_out, b_out, g_out = refs

    h = h_ref[...]
    xx = hp_ref[...] - h
    mix = mix_ref[...]

    def xm(s):
        return (h + xx * mix[s:s + 1]).astype(BF16)

    r = _dot(xm(0), wrkv_ref[0])
    k = _dot(xm(1), wrkv_ref[1])
    x2 = xm(2)
    v = _dot(x2, wrkv_ref[2])
    wl = w0_ref[...] + _dot(jnp.tanh(_dot(xm(3), w1_ref[...])).astype(BF16), w2_ref[...])
    decay = jnp.exp(-jnp.exp(-_softplus(-wl) - 0.5))
    if has_vres:
        mv = jax.nn.sigmoid(v0_ref[...] + _dot(_dot(x2, v1_ref[...]).astype(BF16), v2_ref[...]))
        v = v + (vf_ref[...] - v) * mv
    a = jax.nn.sigmoid(a0_ref[...] + _dot(_dot(xm(4), a1_ref[...]).astype(BF16), a2_ref[...]))
    g = _dot(jax.nn.sigmoid(_dot(xm(5), g1_ref[...])).astype(BF16), g2_ref[...])
    kkr = k * kk_ref[...]
    ssq = _seg_sum64(kkr * kkr, _ones_bd())
    kk = kkr * lax.rsqrt(jnp.maximum(ssq, 1e-24))
    r_out[...] = r
    w_out[...] = decay
    k_out[...] = k * (1.0 + (a - 1.0) * ka_ref[...])
    v_out[...] = v
    a_out[...] = -kk
    b_out[...] = kk * a
    g_out[...] = g


def _pad_to(x, axis, size):
    pad = [(0, 0)] * x.ndim
    pad[axis] = (0, size - x.shape[axis])
    return jnp.pad(x, pad)


def rwkv_project(h, hp, P, l, v_first):
    n, d = h.shape
    tm = _tile_rows(n, 256)
    has_vres = v_first is not None
    row = lambda z: z.reshape(1, d)
    lora_in = lambda w: _pad_to(w, 1, -(-w.shape[1] // LANES) * LANES).astype(BF16)
    lora_out = lambda w: _pad_to(w, 0, -(-w.shape[0] // LANES) * LANES).astype(BF16)
    consts = [_pad_to(P['rwkv_mix'][l], 0, SUBLANES), P['rwkv_w_rkv'][l].astype(BF16),
              row(P['rwkv_w0'][l]), lora_in(P['rwkv_w1'][l]), lora_out(P['rwkv_w2'][l]),
              row(P['rwkv_a0'][l]), lora_in(P['rwkv_a1'][l]), lora_out(P['rwkv_a2'][l]),
              lora_in(P['rwkv_g1'][l]), lora_out(P['rwkv_g2'][l]),
              row(P['rwkv_k_k'][l]), row(P['rwkv_k_a'][l])]
    if has_vres:
        consts += [row(P['rwkv_v0'][l - 1]), lora_in(P['rwkv_v1'][l - 1]), lora_out(P['rwkv_v2'][l - 1])]
    tile = pl.BlockSpec((tm, d), lambda i: (i, 0))
    ins = [h, hp] + consts + ([v_first] if has_vres else [])
    in_specs = [tile, tile] + [_full_spec(c) for c in consts] + ([tile] if has_vres else [])
    return pl.pallas_call(
        functools.partial(_rwkv_proj_body, has_vres),
        out_shape=[jax.ShapeDtypeStruct((n, d), F32)] * 7,
        grid=(n // tm,),
        in_specs=in_specs,
        out_specs=[tile] * 7,
        compiler_params=_cp("arbitrary"),
        name="rwkv_project",
    )(*ins)


def _wkv_body(bb_n, tc_n, r_ref, w_ref, k_ref, v_ref, a_ref, b_ref, s0_ref, y_ref, st_ref, s_scr, y_scr):
    tc = pl.program_id(1)
    n = RWKV_HEAD_DIM
    pairs = bb_n * SUBLANES
    rows = pairs * n

    @pl.when(tc == 0)
    def _():
        for bb in range(bb_n):
            for p in range(SUBLANES):
                r0 = (bb * SUBLANES + p) * n
                s_scr[r0:r0 + n, 0:n] = s0_ref[bb, 2 * p]
                s_scr[r0:r0 + n, n:2 * n] = s0_ref[bb, 2 * p + 1]

    ones_bd = _ones_bd()
    diag = _iota((rows, LANES), 0) % n == _iota((rows, LANES), 1) % n
    grp = min(SUBLANES, tc_n)

    def read_out(yb, i):
        ysum = jnp.sum(jnp.where(diag, yb, 0.0).reshape(pairs, n, LANES), axis=1)
        for bb in range(bb_n):
            for p in range(SUBLANES):
                idx = bb * SUBLANES + p
                y_scr[bb, i:i + 1, p * LANES:(p + 1) * LANES] = ysum[idx:idx + 1]

    def group(gi, carry):
        base = pl.multiple_of(gi * grp, grp)
        tiles = [[ref[bb, pl.ds(base, grp), :] for bb in range(bb_n)]
                 for ref in (r_ref, w_ref, k_ref, v_ref, a_ref, b_ref)]

        def bcast(q, i):
            return jnp.concatenate(
                [jnp.broadcast_to(tiles[q][bb][i:i + 1, p * LANES:(p + 1) * LANES], (n, LANES))
                 for bb in range(bb_n) for p in range(SUBLANES)], axis=0)

        s = s_scr[...]
        for i in range(grp):
            parts = [(s * bcast(4, i)).astype(BF16), jnp.where(diag, bcast(3, i), 0.0).astype(BF16)]
            if i > 0:
                parts.append((s * bcast(0, i - 1)).astype(BF16))
            res = _dot(jnp.concatenate(parts, axis=0), ones_bd)
            if i > 0:
                read_out(res[2 * rows:3 * rows], i - 1)
            s = s * bcast(1, i) + res[0:rows] * bcast(5, i) + res[rows:2 * rows] * bcast(2, i)
        read_out(_dot((s * bcast(0, grp - 1)).astype(BF16), ones_bd), grp - 1)
        s_scr[...] = s
        for bb in range(bb_n):
            y_ref[bb, pl.ds(base, grp), :] = y_scr[bb]
        return carry

    lax.fori_loop(0, tc_n // grp, group, 0)

    @pl.when(tc == pl.num_programs(1) - 1)
    def _():
        for bb in range(bb_n):
            for p in range(SUBLANES):
                r0 = (bb * SUBLANES + p) * n
                st_ref[bb, 2 * p] = s_scr[r0:r0 + n, 0:n]
                st_ref[bb, 2 * p + 1] = s_scr[r0:r0 + n, n:2 * n]


def wkv_scan(r, w, k, v, a, b, s0, bsz, t_len, bb_n, tc_n):
    d = D_MODEL
    n = RWKV_HEAD_DIM
    shp = lambda z: z.reshape(bsz, t_len, d)
    tok = pl.BlockSpec((bb_n, tc_n, d), lambda i, j: (i, j, 0))
    st = pl.BlockSpec((bb_n, RWKV_HEADS, n, n), lambda i, j: (i, 0, 0, 0))
    y, s_t = pl.pallas_call(
        functools.partial(_wkv_body, bb_n, tc_n),
        out_shape=[jax.ShapeDtypeStruct((bsz, t_len, d), F32), jax.ShapeDtypeStruct(s0.shape, F32)],
        grid=(bsz // bb_n, t_len // tc_n),
        in_specs=[tok] * 6 + [st],
        out_specs=[tok, st],
        scratch_shapes=[pltpu.VMEM((bb_n * SUBLANES * n, LANES), F32),
                        pltpu.VMEM((bb_n, min(SUBLANES, tc_n), d), F32)],
        compiler_params=_cp("arbitrary", "arbitrary"),
        name="wkv_scan",
    )(shp(r), shp(w), shp(k), shp(v), shp(a), shp(b), s0)
    return y.reshape(bsz * t_len, d), s_t


def _rwkv_out_body(y_ref, r_ref, k_ref, v_ref, g_ref, x_ref, gt_ref, lw_ref, lb_ref, rk_ref, wo_ref, o_ref):
    ones_bd = _ones_bd()
    y = y_ref[...]
    inv_n = 1.0 / RWKV_HEAD_DIM
    mu = _seg_sum64(y, ones_bd) * inv_n
    yc = y - mu
    var = _seg_sum64(yc * yc, ones_bd) * inv_n
    yn = yc * lax.rsqrt(var + LNX_EPS) * lw_ref[...] + lb_ref[...]
    bonus = _seg_sum64(r_ref[...] * k_ref[...] * rk_ref[...], ones_bd) * v_ref[...]
    out = _dot(((yn + bonus) * g_ref[...]).astype(BF16), wo_ref[...])
    o_ref[...] = x_ref[...] + gt_ref[0] * out


def rwkv_output(y, r, k, v, g, x, gt, P, l, t_len):
    n, d = x.shape
    tm = _tile_rows(n, 256)
    gt_a, gt_s = _mod_spec(gt, t_len, tm)
    row = lambda z: z.reshape(1, d)
    consts = [row(P['rwkv_lnx_w'][l]), row(P['rwkv_lnx_b'][l]), row(P['rwkv_r_k'][l]), P['rwkv_w_o'][l].astype(BF16)]
    tile = pl.BlockSpec((tm, d), lambda i: (i, 0))
    return pl.pallas_call(
        _rwkv_out_body,
        out_shape=jax.ShapeDtypeStruct((n, d), F32),
        grid=(n // tm,),
        in_specs=[tile] * 6 + [gt_s] + [_full_spec(c) for c in consts],
        out_specs=tile,
        compiler_params=_cp("arbitrary"),
        name="rwkv_output",
    )(y, r, k, v, g, x, gt_a, *consts)


BIGNEG = -3.0e38


def _route(logits):
    lane = _iota(logits.shape, 1).astype(F32)
    gmask = lane < N_GROUPS
    gmax = jnp.max(jnp.where(gmask, logits, BIGNEG), axis=-1, keepdims=True)
    grp = jnp.min(jnp.where(gmask & (logits == gmax), lane, 999.0), axis=-1, keepdims=True)
    denom = jnp.sum(jnp.where(gmask, jnp.exp(jnp.where(gmask, logits - gmax, 0.0)), 0.0), axis=-1, keepdims=True)
    p_grp = 1.0 / denom
    lo = N_GROUPS + GROUP_EXPERTS * grp
    inmask = (lane >= lo) & (lane < lo + GROUP_EXPERTS)
    m1 = jnp.max(jnp.where(inmask, logits, BIGNEG), axis=-1, keepdims=True)
    i1 = jnp.min(jnp.where(inmask & (logits == m1), lane, 999.0), axis=-1, keepdims=True)
    mask2 = inmask & (lane != i1)
    m2 = jnp.max(jnp.where(mask2, logits, BIGNEG), axis=-1, keepdims=True)
    i2 = jnp.min(jnp.where(mask2 & (logits == m2), lane, 999.0), axis=-1, keepdims=True)
    e12 = jnp.exp(m2 - m1)
    w1 = p_grp / (1.0 + e12)
    w2 = p_grp * e12 / (1.0 + e12)
    return jnp.where(lane == 0, i1 - N_GROUPS,
                     jnp.where(lane == 1, i2 - N_GROUPS, jnp.where(lane == 2, w1, jnp.where(lane == 3, w2, 0.0))))


def _moe_body(x_ref, g_ref, sh_ref, sc_ref, gt_ref, wr_ref, br_ref, wg_ref, wu_ref, wd_ref, o_ref,
              h_scr, route_scr, acc_scr):
    e = pl.program_id(1)

    @pl.when(e == 0)
    def _():
        h = _norm_mod(x_ref[...], g_ref[...], sh_ref[0], sc_ref[0])
        h_scr[...] = h.astype(BF16)
        route_scr[...] = _route(_dot(h_scr[...], wr_ref[...].astype(BF16)) + br_ref[...])
        acc_scr[...] = jnp.zeros_like(acc_scr)

    h = h_scr[...]
    act = _silu(_dot(h, wg_ref[0].astype(BF16))) * _dot(h, wu_ref[0].astype(BF16))
    rt = route_scr[...]
    ef = e.astype(F32)
    coef = jnp.where(rt[:, 0:1] == ef, rt[:, 2:3], 0.0) + jnp.where(rt[:, 1:2] == ef, rt[:, 3:4], 0.0)
    acc_scr[...] += _dot((act * coef).astype(BF16), wd_ref[0].astype(BF16))

    @pl.when(e == N_EXPERTS - 1)
    def _():
        o_ref[...] = x_ref[...] + gt_ref[0] * acc_scr[...]


def moe_dense(x, sh, sc, gt, P, l, t_len):
    n, d = x.shape
    tm = _tile_rows(n, 1024)
    mods = [_mod_spec(z, t_len, tm) for z in (sh, sc, gt)]
    wr = _pad_to(jnp.concatenate([P['moe_w_grp'][l], P['moe_w_exp'][l]], axis=1), 1, LANES)
    br = _pad_to(jnp.concatenate([P['moe_b_grp'][l], P['moe_b_exp'][l]]), 0, LANES).reshape(1, LANES)
    tile = pl.BlockSpec((tm, d), lambda i, e: (i, 0))
    ff = EXPERT_FF
    return pl.pallas_call(
        _moe_body,
        out_shape=jax.ShapeDtypeStruct((n, d), F32),
        grid=(n // tm, N_EXPERTS),
        in_specs=[tile, pl.BlockSpec((1, d), lambda i, e: (0, 0))] + [m[1] for m in mods]
        + [pl.BlockSpec((d, LANES), lambda i, e: (0, 0)), pl.BlockSpec((1, LANES), lambda i, e: (0, 0)),
           pl.BlockSpec((None, 1, d, ff), lambda i, e: (l, e, 0, 0)),
           pl.BlockSpec((None, 1, d, ff), lambda i, e: (l, e, 0, 0)),
           pl.BlockSpec((None, 1, ff, d), lambda i, e: (l, e, 0, 0))],
        out_specs=tile,
        scratch_shapes=[pltpu.VMEM((tm, d), BF16), pltpu.VMEM((tm, LANES), F32), pltpu.VMEM((tm, d), F32)],
        compiler_params=_cp("arbitrary", "arbitrary"),
        name="moe_dense",
    )(x, P['norm_ffn_g'][l].reshape(1, d), *[m[0] for m in mods], wr, br,
      P['moe_w_gate'], P['moe_w_up'], P['moe_w_down'])


def _nm_matmul_body(x_ref, g_ref, sh_ref, sc_ref, w_ref, o_ref):
    h = _norm_mod(x_ref[...], g_ref[...], sh_ref[0], sc_ref[0])
    o_ref[...] = _dot(h.astype(BF16), w_ref[...])


def norm_mod_matmul(x, g, sh, sc, w, t_len):
    n, d = x.shape
    nout = w.shape[1]
    tm = _tile_rows(n, 512)
    sh_a, sh_s = _mod_spec(sh, t_len, tm)
    sc_a, sc_s = _mod_spec(sc, t_len, tm)
    return pl.pallas_call(
        _nm_matmul_body,
        out_shape=jax.ShapeDtypeStruct((n, nout), F32),
        grid=(n // tm,),
        in_specs=[pl.BlockSpec((tm, d), lambda i: (i, 0)), pl.BlockSpec((1, d), lambda i: (0, 0)), sh_s, sc_s,
                  pl.BlockSpec((d, nout), lambda i: (0, 0))],
        out_specs=pl.BlockSpec((tm, nout), lambda i: (i, 0)),
        compiler_params=_cp("arbitrary"),
        name="norm_mod_matmul",
    )(x, g.reshape(1, d), sh_a, sc_a, w.astype(BF16))


def _kv_proj_body(x_ref, g_ref, sh_ref, sc_ref, w_ref, wt_ref, o_ref, ot_ref):
    h = _norm_mod(x_ref[...], g_ref[...], sh_ref[0], sc_ref[0]).astype(BF16)
    o_ref[...] = _dot(h, w_ref[...])
    ot_ref[0] = _dot_nt(wt_ref[...], h).astype(BF16)


def kv_project_with_transpose(x, g, sh, sc, w, n_t, bsz, t_len):
    n, d = x.shape
    nout = w.shape[1]
    tm = _tile_rows(t_len, 512)
    tpb = t_len // tm
    sh_a, sh_s = _mod_spec(sh, t_len, tm)
    sc_a, sc_s = _mod_spec(sc, t_len, tm)
    wb = w.astype(BF16)
    return pl.pallas_call(
        _kv_proj_body,
        out_shape=[jax.ShapeDtypeStruct((n, nout), F32), jax.ShapeDtypeStruct((bsz, n_t, t_len), BF16)],
        grid=(n // tm,),
        in_specs=[pl.BlockSpec((tm, d), lambda i: (i, 0)), pl.BlockSpec((1, d), lambda i: (0, 0)), sh_s, sc_s,
                  pl.BlockSpec((d, nout), lambda i: (0, 0)), pl.BlockSpec((n_t, d), lambda i: (0, 0))],
        out_specs=[pl.BlockSpec((tm, nout), lambda i: (i, 0)),
                   pl.BlockSpec((1, n_t, tm), lambda i: (i // tpb, 0, i % tpb))],
        compiler_params=_cp("arbitrary"),
        name="kv_project",
    )(x, g.reshape(1, d), sh_a, sc_a, wb, wb[:, nout - n_t:].T)


def _gelu_tanh(x):
    return 0.5 * x * (1.0 + jnp.tanh(math.sqrt(2.0 / math.pi) * (x + 0.044715 * (x * x * x))))


def _compress_body(n_blk, rows_per_blk, slabs, n_prefetch, *refs):
    refs = refs[n_prefetch:]
    row_refs = refs[:n_blk]
    pos_ref, w1_ref, b1_ref, w2_ref, o_ref, x_scr = refs[n_blk:n_blk + 6]
    hd = NSA_HEAD_DIM
    cpb = rows_per_blk // CMP_STRIDE
    nch = n_blk * cpb
    for z in range(2):
        for j in range(n_blk):
            for gp in range(NSA_KV_HEADS // 2):
                if slabs == 0:
                    tr_scr = refs[n_blk + 6]
                    c0 = (z * 2 + gp) * LANES
                    tr_scr[...] = row_refs[j][0, c0:c0 + LANES, :].T
                for s in range(CMP_STRIDE):
                    if slabs == 0:
                        src = tr_scr[pl.ds(s, cpb, stride=CMP_STRIDE), :]
                    else:
                        src = row_refs[j][0, pl.ds(s * slabs + z * 2 + gp, cpb, stride=CMP_STRIDE * slabs), :]
                    for gl in range(2):
                        g = 2 * gp + gl
                        x_scr[g * nch + j * cpb:g * nch + (j + 1) * cpb, s * hd:(s + 1) * hd] = (
                            src[:, gl * hd:(gl + 1) * hd])
        w1 = w1_ref[z]
        part = _dot(x_scr[...].astype(BF16), w1)
        pp = _dot_x2(pos_ref[z], w1)
        posb = pp[0:1, :CMP_HIDDEN] + pp[1:2, CMP_HIDDEN:] + b1_ref[z]
        w2 = w2_ref[z]
        for g in range(NSA_KV_HEADS):
            pg = part[g * nch:(g + 1) * nch]
            nxt = pltpu.roll(pg[:, CMP_HIDDEN:], nch - 1, 0)
            hid = posb + pg[:, :CMP_HIDDEN] + nxt
            o_ref[z, 0, g] = _dot(_gelu_tanh(hid).astype(BF16), w2)


def _compress_consts(cmp_pos, cmp_w1, cmp_b1, cmp_w2):
    hd = NSA_HEAD_DIM
    k_half = CMP_STRIDE * hd
    w1 = cmp_w1.reshape(2, 2, k_half, CMP_HIDDEN)
    w1cat = jnp.concatenate([w1[:, 0], w1[:, 1]], axis=-1).astype(BF16)
    pos = _pad_to(cmp_pos.reshape(2, 2, k_half), 1, SUBLANES)
    return pos, w1cat, cmp_b1.reshape(2, 1, CMP_HIDDEN), cmp_w2.astype(BF16)


def compress_call(row_arrays, row_specs, grid, prefetch, bsz, nch, rows_per_blk, slabs, consts):
    n_blk = len(row_arrays)
    np_ = 1 if prefetch is not None else 0
    cspecs = [pl.BlockSpec(c.shape, (lambda *_, nd=c.ndim: (0,) * nd)) for c in consts]
    gs = pltpu.PrefetchScalarGridSpec(
        num_scalar_prefetch=np_, grid=grid, in_specs=list(row_specs) + cspecs,
        out_specs=pl.BlockSpec((2, 1, NSA_KV_HEADS, nch, NSA_HEAD_DIM), lambda b, *_: (0, b, 0, 0, 0)),
        scratch_shapes=[pltpu.VMEM((NSA_KV_HEADS * nch, CMP_STRIDE * NSA_HEAD_DIM), F32)]
        + ([pltpu.VMEM((rows_per_blk, LANES), F32)] if slabs == 0 else []))
    args = ([prefetch] if np_ else []) + list(row_arrays) + list(consts)
    return pl.pallas_call(
        functools.partial(_compress_body, n_blk, rows_per_blk, slabs, np_),
        out_shape=jax.ShapeDtypeStruct((2, bsz, NSA_KV_HEADS, nch, NSA_HEAD_DIM), F32),
        grid_spec=gs, compiler_params=_cp("arbitrary"), name="nsa_compress",
    )(*args)


def _masked_softmax(s, mask):
    s = jnp.where(mask, s, NEG)
    e = jnp.exp(s - jnp.max(s, axis=-1, keepdims=True))
    return jnp.where(mask, e / jnp.sum(e, axis=-1, keepdims=True), 0.0)


def _overlap(nch, nbp):
    n = _iota((nch, nbp), 0) * CMP_STRIDE
    j = _iota((nch, nbp), 1) * SEL_BLOCK
    return ((n <= j + SEL_BLOCK - 1) & (n + CMP_LEN - 1 >= j)).astype(BF16)


def _compressed_branch(qs, kc, vc, slope, trow, nc_valid, n_tok):
    nch = kc.shape[0]
    ncol = _iota((1, nch), 1)
    d_c = trow - (ncol * CMP_STRIDE + (CMP_LEN - 1))
    m_c = (d_c >= 0) & (ncol < nc_valid)
    s_c = _dot_nt(qs, kc.astype(BF16)) - slope * d_c.astype(F32)
    p_c = _masked_softmax(s_c, m_c)
    o_c = _dot(p_c.astype(BF16), vc.astype(BF16))
    psum = p_c[0:n_tok]
    for p in range(1, NSA_GROUP):
        psum = psum + p_c[p * n_tok:(p + 1) * n_tok]
    return o_c, psum


def _select_blocks(imp, tpos, nb):
    nbp = imp.shape[1]
    blk = _iota((1, nbp), 1)
    cur = tpos // SEL_BLOCK
    forced = (blk == 0) | (blk == cur) | (blk == cur - 1)
    score = jnp.where(blk * SEL_BLOCK <= tpos, jnp.where(forced, -NEG, imp), NEG)
    score = jnp.where(blk < nb, score, 2.0 * NEG)
    rank = jnp.zeros(imp.shape, F32)
    blk_f = blk.astype(F32)
    for i in range(nb):
        ci = score[:, i:i + 1]
        tie = jnp.where(blk_f > float(i), 1.0, 0.0)
        rank = rank + jnp.where(ci > score, 1.0, jnp.where(ci == score, tie, 0.0))
    return jnp.where(rank < float(min(SEL_TOP, nb)), 1.0, 0.0)


def _select_blocks_t(imp_t, tpos, nb):
    blk = _iota(imp_t.shape, 0)
    cur = tpos // SEL_BLOCK
    forced = (blk == 0) | (blk == cur) | (blk == cur - 1)
    score = jnp.where(blk * SEL_BLOCK <= tpos, jnp.where(forced, -NEG, imp_t), NEG)
    score = jnp.where(blk < nb, score, 2.0 * NEG)
    rank = jnp.zeros(imp_t.shape, F32)
    for i in range(nb):
        ci = score[i:i + 1, :]
        rank = rank + jnp.where(ci > score, 1.0, jnp.where(ci == score, jnp.where(blk > i, 1.0, 0.0), 0.0))
    return jnp.where(rank < float(min(SEL_TOP, nb)), 1.0, 0.0)


def _expand_sel(sel, k0, tk):
    nbp = sel.shape[1]
    e_mat = (_iota((nbp, tk), 0) == (k0 + _iota((nbp, tk), 1)) // SEL_BLOCK).astype(BF16)
    m = _dot(sel.astype(BF16), e_mat)
    return jnp.concatenate([m] * NSA_GROUP, axis=0)


def _gate_combine(gate, branches, n_tok):
    outs = []
    for p in range(NSA_GROUP):
        acc = None
        for c, o in enumerate(branches):
            term = gate[:, p * 3 + c:p * 3 + c + 1] * o[p * n_tok:(p + 1) * n_tok]
            acc = term if acc is None else acc + term
        outs.append(acc)
    return jnp.concatenate(outs, axis=1)


def _stack_heads(qg, scale):
    hd = NSA_HEAD_DIM
    return jnp.concatenate([qg[:, p * hd:(p + 1) * hd] for p in range(NSA_GROUP)], axis=0) * scale


AUG_BLK = 64
AUG_ROWS = 128


def _key_aug(t_len):
    pos = jnp.arange(t_len, dtype=jnp.int32)[None, :]
    r = jnp.arange(AUG_ROWS, dtype=jnp.int32)[:, None]
    blk = pos // SEL_BLOCK
    onehot = (r == blk) & (r < AUG_BLK)
    hi = (r >= AUG_BLK) & (r < AUG_BLK + 3)
    lo = (r >= AUG_BLK + 3) & (r < AUG_BLK + 6)
    val = jnp.where(onehot, 1, jnp.where(hi, blk * SEL_BLOCK, jnp.where(lo, pos % SEL_BLOCK, 0)))
    return val.astype(BF16)


def _slope_aug(n_tok):
    sl = _alibi_slope_rows(n_tok)
    hi = sl.astype(BF16).astype(F32)
    mid = (sl - hi).astype(BF16).astype(F32)
    lo = (sl - hi - mid).astype(BF16).astype(F32)
    return _pad_to(jnp.concatenate([hi, mid, lo, hi, mid, lo], axis=-1), 2, AUG_BLK)


def _nsa_prompt_body(tq, tk, t_len, q_ref, gate_ref, slope_ref, saug_ref, kaug_ref, causal_ref, reach_ref,
                     kc_ref, vc_ref, ks_ref, vs_ref, kw_ref, vw_ref, o_ref, kcat_s, kcat_w):
    qi = pl.program_id(2)
    q0 = qi * tq
    rows = NSA_GROUP * tq
    nch = t_len // CMP_STRIDE
    nb = t_len // SEL_BLOCK
    assert nb <= AUG_BLK and tk % tq == 0

    @pl.when(qi == 0)
    def _():
        kcat_s[0:AUG_ROWS, :] = kaug_ref[...]
        kcat_s[AUG_ROWS:, :] = ks_ref[0]
        kcat_w[0:AUG_ROWS, :] = kaug_ref[...]
        kcat_w[AUG_ROWS:, :] = kw_ref[0]

    qs = _stack_heads(q_ref[0], NSA_HEAD_DIM ** -0.5).astype(BF16)
    slope = slope_ref[0]
    trow = q0 + _iota((rows, 1), 0) % tq
    tpos = q0 + _iota((tq, 1), 0)

    o_c, psum = _compressed_branch(qs, kc_ref[0, 0, 0], vc_ref[0, 0, 0], slope, trow, nch - 1, tq)
    ov_t = (_iota((AUG_BLK, nch), 1) * CMP_STRIDE <= _iota((AUG_BLK, nch), 0) * SEL_BLOCK + SEL_BLOCK - 1) & (
        _iota((AUG_BLK, nch), 1) * CMP_STRIDE + CMP_LEN - 1 >= _iota((AUG_BLK, nch), 0) * SEL_BLOCK)
    p_hi, p_lo = _split2(psum)
    imp_t = _dot_nt(ov_t.astype(BF16), p_hi) + _dot_nt(ov_t.astype(BF16), p_lo)
    sel_t = _select_blocks_t(imp_t, q0 + _iota((1, tq), 1), nb)
    eye = (_iota((AUG_BLK, AUG_BLK), 0) == _iota((AUG_BLK, AUG_BLK), 1)).astype(BF16)
    sel = lax.dot_general(sel_t.astype(BF16), eye, (((0,), (0,)), ((), ())), preferred_element_type=F32)
    blocks_per_tile = tk // SEL_BLOCK
    n_tiles = AUG_BLK // blocks_per_tile
    tile_any = jnp.max(jnp.max(sel_t, axis=1, keepdims=True).reshape(n_tiles, blocks_per_tile, 1), axis=1)
    tile_idx = _iota((n_tiles, 1), 0).astype(F32)
    first_live = jnp.min(jnp.where(tile_any > 0.5, jnp.where(tile_idx >= 1.0, tile_idx, float(n_tiles)),
                                   float(n_tiles))).astype(jnp.int32)
    selbias = jnp.concatenate([jnp.where(sel > 0.5, 0.0, NEG)] * NSA_GROUP, axis=0)
    saug = saug_ref[0]
    q_sel = jnp.concatenate([selbias, saug], axis=1).astype(BF16)
    q_sel = jnp.concatenate([q_sel, qs], axis=1)
    q_win = jnp.concatenate([jnp.zeros_like(selbias), saug], axis=1).astype(BF16)
    q_win = jnp.concatenate([q_win, qs], axis=1)
    causal = jnp.concatenate([causal_ref[0]] * NSA_GROUP, axis=0)

    def scores(kcat, q_cat, k0):
        return _dot(q_cat, kcat[:, pl.ds(k0, tk)])

    def online(carry, s, v_ref, k0):
        m, l, acc = carry
        m_new = jnp.maximum(m, jnp.max(s, axis=-1, keepdims=True))
        alpha = jnp.exp(m - m_new)
        p = jnp.exp(s - m_new)
        l = alpha * l + jnp.sum(p, axis=-1, keepdims=True)
        acc = alpha * acc + _dot_nt(p.astype(BF16), v_ref[0, :, pl.ds(k0, tk)])
        return m_new, l, acc

    def finish(carry):
        _, l, acc = carry
        return acc * jnp.where(l > 0.0, 1.0 / l, 0.0)

    init = (jnp.full((rows, 1), NEG, F32), jnp.zeros((rows, 1), F32), jnp.zeros((rows, NSA_HEAD_DIM), F32))
    kt_last = q0 // tk
    k_last = pl.multiple_of(kt_last * tk, tk)

    def sel_body(kt, carry):
        k0 = pl.multiple_of(kt * tk, tk)
        return online(carry, scores(kcat_s, q_sel, k0), vs_ref, k0)

    carry = online(init, scores(kcat_s, q_sel, 0) + jnp.where(kt_last > 0, 0.0, NEG), vs_ref, 0)
    carry = lax.fori_loop(jnp.maximum(first_live, 1), kt_last, sel_body, carry)
    o_s = finish(online(carry, scores(kcat_s, q_sel, k_last) + causal, vs_ref, k_last))

    carry = online(init, scores(kcat_w, q_win, k_last) + causal, vw_ref, k_last)
    k_prev = pl.multiple_of(jnp.maximum(kt_last - 1, 0) * tk, tk)
    reach = jnp.where(kt_last > 0, jnp.concatenate([reach_ref[0]] * NSA_GROUP, axis=0), NEG)
    o_w = finish(online(carry, scores(kcat_w, q_win, k_prev) + reach, vw_ref, k_prev))

    gate = jax.nn.sigmoid(gate_ref[0, 0])
    o_ref[0] = _gate_combine(gate, (o_c, o_s, o_w), tq)


def _alibi_slope_rows(n_tok):
    h = jnp.arange(1, NSA_HEADS + 1, dtype=F32)
    sl = jnp.exp2(-8.0 * h / NSA_HEADS).reshape(NSA_KV_HEADS, NSA_GROUP, 1)
    return jnp.broadcast_to(sl, (NSA_KV_HEADS, NSA_GROUP, n_tok)).reshape(NSA_KV_HEADS, NSA_GROUP * n_tok, 1)


def _gate_array(qg, bsz, t_len):
    gates = qg[:, NSA_HEADS * NSA_HEAD_DIM:NSA_HEADS * NSA_HEAD_DIM + 3 * NSA_HEADS]
    gates = gates.reshape(bsz, t_len, NSA_KV_HEADS, 3 * NSA_GROUP).transpose(0, 2, 1, 3)
    return _pad_to(gates, 3, LANES)


def nsa_prompt_attend(qg, cmp_kv, kvt, bsz, t_len):
    tq = min(128, t_len)
    tk = min(512, t_len)
    gq = NSA_GROUP * NSA_HEAD_DIM
    hd = NSA_HEAD_DIM
    nch = cmp_kv.shape[3]
    q3 = qg.reshape(bsz, t_len, qg.shape[1])
    full_t = lambda z: pl.BlockSpec((1, hd, t_len), lambda b, g, i, z=z: (b, z * NSA_KV_HEADS + g, 0))
    cmp_spec = lambda z: pl.BlockSpec((1, 1, 1, nch, hd), lambda b, g, i, z=z: (z, b, g, 0, 0))
    rows = NSA_GROUP * tq
    assert tk == WINDOW or t_len <= WINDOW
    n_pat = tk // tq
    off = (jnp.arange(n_pat)[:, None, None] * tq + jnp.arange(tq)[None, :, None])
    col = jnp.arange(tk)[None, None, :]
    causal_pat = jnp.where(col <= off, 0.0, NEG).astype(F32)
    reach_pat = jnp.where(col > off, 0.0, NEG).astype(F32)
    pat_spec = pl.BlockSpec((1, tq, tk), lambda b, g, i: (i % n_pat, 0, 0))

    out = pl.pallas_call(
        functools.partial(_nsa_prompt_body, tq, tk, t_len),
        out_shape=jax.ShapeDtypeStruct((bsz, t_len, NSA_HEADS * hd), F32),
        grid=(bsz, NSA_KV_HEADS, t_len // tq),
        in_specs=[pl.BlockSpec((1, tq, gq), lambda b, g, i: (b, i, g)),
                  pl.BlockSpec((1, 1, tq, LANES), lambda b, g, i: (b, g, i, 0)),
                  pl.BlockSpec((1, rows, 1), lambda b, g, i: (g, 0, 0)),
                  pl.BlockSpec((1, rows, AUG_BLK), lambda b, g, i: (g, 0, 0)),
                  pl.BlockSpec((AUG_ROWS, t_len), lambda b, g, i: (0, 0)), pat_spec, pat_spec,
                  cmp_spec(0), cmp_spec(1), full_t(0), full_t(1), full_t(2), full_t(3)],
        out_specs=pl.BlockSpec((1, tq, gq), lambda b, g, i: (b, i, g)),
        scratch_shapes=[pltpu.VMEM((AUG_ROWS + hd, t_len), BF16), pltpu.VMEM((AUG_ROWS + hd, t_len), BF16)],
        compiler_params=_cp("arbitrary", "arbitrary", "arbitrary"),
        name="nsa_prompt_attend",
    )(q3, _gate_array(qg, bsz, t_len), _alibi_slope_rows(tq), _slope_aug(tq), _key_aug(t_len),
      causal_pat, reach_pat, cmp_kv, cmp_kv, kvt, kvt, kvt, kvt)
    return out.reshape(bsz * t_len, NSA_HEADS * hd)


def _nsa_sample_body(n_pages, page_rows, s_len, wb, pt_ref, q_ref, gate_ref, slope_ref, kc_ref, vc_ref, *refs):
    page_refs = refs[:n_pages]
    new_ref, win_ref, o_ref = refs[n_pages:]
    del pt_ref
    hd = NSA_HEAD_DIM
    gq = NSA_GROUP * hd
    past_len = n_pages * page_rows
    rows = NSA_GROUP * s_len
    nch = kc_ref.shape[3]
    nb = -(-(past_len + s_len) // SEL_BLOCK)
    nbp = 64
    assert nb <= nbp and page_rows % SEL_BLOCK == 0
    trow = past_len + _iota((rows, 1), 0) % s_len
    tpos = past_len + _iota((s_len, 1), 0)
    tloc = _iota((rows, 1), 0) % s_len
    q_all = q_ref[0]
    new = new_ref[0]
    outs = []
    for g in range(NSA_KV_HEADS):
        qs = _stack_heads(q_all[:, g * gq:(g + 1) * gq], hd ** -0.5)
        qsb = qs.astype(BF16)
        slope = slope_ref[g]
        o_c, psum = _compressed_branch(qsb, kc_ref[0, 0, g], vc_ref[0, 0, g], slope, trow, nch - 1, s_len)
        imp = _dot_x2(psum, _overlap(nch, nbp))
        sel = _select_blocks(imp, tpos, nb)

        def new_scores(z):
            knew = new[:, z * gq + g * hd:z * gq + (g + 1) * hd]
            cols = []
            for t2 in range(s_len):
                sc = jnp.sum(qs * knew[t2:t2 + 1], axis=-1, keepdims=True)
                cols.append((sc - slope * (tloc - t2).astype(F32), tloc >= t2))
            return cols

        def finish(tiles, cols, vals, vnew):
            m = jnp.full((rows, 1), NEG, F32)
            for s, mk in tiles + cols:
                m = jnp.maximum(m, jnp.max(jnp.where(mk, s, NEG), axis=-1, keepdims=True))
            l = jnp.zeros((rows, 1), F32)
            acc = jnp.zeros((rows, hd), F32)
            for (s, mk), v in zip(tiles, vals):
                p = jnp.where(mk, jnp.exp(jnp.where(mk, s, NEG) - m), 0.0)
                l = l + jnp.sum(p, axis=-1, keepdims=True)
                acc = acc + _dot_nt(p.astype(BF16), v.astype(BF16))
            for t2, (s, mk) in enumerate(cols):
                p = jnp.where(mk, jnp.exp(jnp.where(mk, s, NEG) - m), 0.0)
                l = l + p
                acc = acc + p * vnew[t2:t2 + 1]
            return acc * jnp.where(l > 0.0, 1.0 / l, 0.0)

        k_past = jnp.concatenate([pr[0, g * hd:(g + 1) * hd, :].astype(BF16) for pr in page_refs], axis=1)
        v_past = jnp.concatenate([pr[0, gq + g * hd:gq + (g + 1) * hd, :].astype(BF16) for pr in page_refs], axis=1)
        d = trow - _iota((1, past_len), 1)
        s = _dot(qsb, k_past) - slope * d.astype(F32)
        tiles = [(s, (d >= 0) & (_expand_sel(sel, 0, past_len) > 0.5))]
        vals = [v_past]
        cols = new_scores(2)
        blk_new = past_len // SEL_BLOCK
        sel_new = jnp.concatenate([sel[:, blk_new:blk_new + 1]] * NSA_GROUP, axis=0) > 0.5
        cols = [(s, mk & sel_new) for s, mk in cols]
        o_s = finish(tiles, cols, vals, new[:, 3 * gq + g * hd:3 * gq + (g + 1) * hd])

        s = _dot(qsb, win_ref[0, g * hd:(g + 1) * hd, :].astype(BF16))
        d = trow - (past_len - wb + _iota((1, wb), 1))
        s = s - slope * d.astype(F32)
        cols = [(sc, mk) for sc, mk in new_scores(4)]
        o_w = finish([(s, (d >= 0) & (d < WINDOW))], cols, [win_ref[0, gq + g * hd:gq + (g + 1) * hd, :]],
                     new[:, 5 * gq + g * hd:5 * gq + (g + 1) * hd])
        outs.append(_gate_combine(jax.nn.sigmoid(gate_ref[0, g]), (o_c, o_s, o_w), s_len))
    o_ref[0] = jnp.concatenate(outs, axis=1)


def nsa_sample_attend(qg, cmp_kv, cache_pages, page_table, cache_win, kv_new, bsz, s_len):
    n_pages = page_table.shape[1]
    page_rows = cache_pages.shape[2]
    wb = cache_win.shape[2]
    gq = NSA_GROUP * NSA_HEAD_DIM
    half = 2 * NSA_KV_HEADS * NSA_HEAD_DIM
    nch = cmp_kv.shape[3]
    q3 = qg.reshape(bsz, s_len, qg.shape[1])
    cmp_spec = lambda z: pl.BlockSpec((1, 1, NSA_KV_HEADS, nch, NSA_HEAD_DIM), lambda b, pt, z=z: (z, b, 0, 0, 0))
    page_specs = [pl.BlockSpec((1, half, page_rows), lambda b, pt, j=j: (pt[b, j], 1, 0)) for j in range(n_pages)]
    gs = pltpu.PrefetchScalarGridSpec(
        num_scalar_prefetch=1, grid=(bsz,),
        in_specs=[pl.BlockSpec((1, s_len, q3.shape[2]), lambda b, pt: (b, 0, 0)),
                  pl.BlockSpec((1, NSA_KV_HEADS, s_len, LANES), lambda b, pt: (b, 0, 0, 0)),
                  pl.BlockSpec((NSA_KV_HEADS, NSA_GROUP * s_len, 1), lambda b, pt: (0, 0, 0)),
                  cmp_spec(0), cmp_spec(1)] + page_specs
        + [pl.BlockSpec((1, s_len, kv_new.shape[1]), lambda b, pt: (b, 0, 0)),
           pl.BlockSpec((1, half, wb), lambda b, pt: (b, 0, 0))],
        out_specs=pl.BlockSpec((1, s_len, NSA_HEADS * NSA_HEAD_DIM), lambda b, pt: (b, 0, 0)))
    out = pl.pallas_call(
        functools.partial(_nsa_sample_body, n_pages, page_rows, s_len, wb),
        out_shape=jax.ShapeDtypeStruct((bsz, s_len, NSA_HEADS * NSA_HEAD_DIM), F32),
        grid_spec=gs, compiler_params=_cp("arbitrary"), name="nsa_sample_attend",
    )(page_table, q3, _gate_array(qg, bsz, s_len), _alibi_slope_rows(s_len), cmp_kv, cmp_kv,
      *([cache_pages] * n_pages), kv_new.reshape(bsz, s_len, kv_new.shape[1]), cache_win)
    return out.reshape(bsz * s_len, NSA_HEADS * NSA_HEAD_DIM)


def _resid_matmul_body(o_ref, x_ref, gt_ref, w_ref, out_ref):
    out_ref[...] = x_ref[...] + gt_ref[0] * _dot(o_ref[...].astype(BF16), w_ref[...])


def resid_matmul(o, x, gt, w, t_len):
    n, d = x.shape
    tm = _tile_rows(n, 512)
    gt_a, gt_s = _mod_spec(gt, t_len, tm)
    tile = pl.BlockSpec((tm, d), lambda i: (i, 0))
    return pl.pallas_call(
        _resid_matmul_body,
        out_shape=jax.ShapeDtypeStruct((n, d), F32),
        grid=(n // tm,),
        in_specs=[pl.BlockSpec((tm, o.shape[1]), lambda i: (i, 0)), tile, gt_s, _full_spec(w)],
        out_specs=tile,
        compiler_params=_cp("arbitrary"),
        name="resid_matmul",
    )(o, x, gt_a, w.astype(BF16))


def _run_trunk(x, mods, kvmod, shift0, wkv0, P, bsz, t_len, make_attend, scan_bb, scan_tc):
    d = D_MODEL
    v_first, attend, kv = None, None, None
    shifts, states = [], []
    for l in range(DEPTH):
        sh1, sc1, gt1, sh2, sc2, gt2 = [mods[l][:, i * d:(i + 1) * d] for i in range(6)]
        if l < N_A:
            h = norm_mod(x, P['norm_mix_g'][l], sh1, sc1, t_len)
            h3 = h.reshape(bsz, t_len, d)
            hp = jnp.concatenate([shift0[l][:, None, :], h3[:, :-1]], axis=1).reshape(bsz * t_len, d)
            r, w, k, v, a, b, g = rwkv_project(h, hp, P, l, v_first)
            if v_first is None:
                v_first = v
            y, s_t = wkv_scan(r, w, k, v, a, b, wkv0[l], bsz, t_len, scan_bb, scan_tc)
            x = rwkv_output(y, r, k, v, g, x, gt1, P, l, t_len)
            shifts.append(h3[:, -1])
            states.append(s_t)
        else:
            w_qg = _pad_to(P['nsa_w_qg'][l - N_A], 1, 9 * LANES)
            qg = norm_mod_matmul(x, P['norm_mix_g'][l], sh1, sc1, w_qg, t_len)
            x = resid_matmul(attend(qg), x, gt1, P['nsa_w_o'][l - N_A], t_len)
        x = moe_dense(x, sh2, sc2, gt2, P, l, t_len)
        if l == N_A - 1:
            kv, attend = make_attend(x, kvmod[:, :d], kvmod[:, d:])
    return rms_norm_final(x, P['norm_f_g']), kv, jnp.stack(shifts), jnp.stack(states)


def kernel(x_prompt, x_sample, cache_nsa_kv, cache_win_kv, state_wkv, state_shift, page_table, c_prompt, c_sample, ada_w, ada_b, norm_mix_g, norm_ffn_g, norm_f_g, rwkv_mix, rwkv_w_rkv, rwkv_w0, rwkv_w1, rwkv_w2, rwkv_a0, rwkv_a1, rwkv_a2, rwkv_v0, rwkv_v1, rwkv_v2, rwkv_g1, rwkv_g2, rwkv_k_k, rwkv_k_a, rwkv_r_k, rwkv_lnx_w, rwkv_lnx_b, rwkv_w_o, kv_norm_g, kv_ada_w, kv_ada_b, w_kv, cmp_pos, cmp_w1, cmp_b1, cmp_w2, nsa_w_qg, nsa_w_o, moe_w_grp, moe_b_grp, moe_w_exp, moe_b_exp, moe_w_gate, moe_w_up, moe_w_down):
    P = dict(norm_mix_g=norm_mix_g, norm_ffn_g=norm_ffn_g, norm_f_g=norm_f_g,
             rwkv_mix=rwkv_mix, rwkv_w_rkv=rwkv_w_rkv, rwkv_w0=rwkv_w0, rwkv_w1=rwkv_w1, rwkv_w2=rwkv_w2,
             rwkv_a0=rwkv_a0, rwkv_a1=rwkv_a1, rwkv_a2=rwkv_a2, rwkv_v0=rwkv_v0, rwkv_v1=rwkv_v1, rwkv_v2=rwkv_v2,
             rwkv_g1=rwkv_g1, rwkv_g2=rwkv_g2, rwkv_k_k=rwkv_k_k, rwkv_k_a=rwkv_k_a,
             rwkv_r_k=rwkv_r_k.reshape(N_A, D_MODEL), rwkv_lnx_w=rwkv_lnx_w, rwkv_lnx_b=rwkv_lnx_b,
             rwkv_w_o=rwkv_w_o, kv_norm_g=kv_norm_g, w_kv=w_kv, nsa_w_qg=nsa_w_qg, nsa_w_o=nsa_w_o,
             moe_w_grp=moe_w_grp, moe_b_grp=moe_b_grp, moe_w_exp=moe_w_exp, moe_b_exp=moe_b_exp,
             moe_w_gate=moe_w_gate, moe_w_up=moe_w_up, moe_w_down=moe_w_down)
    d = D_MODEL
    hd = NSA_HEAD_DIM
    g_kv = NSA_KV_HEADS
    bp, t_p, _ = x_prompt.shape
    bs, t_s, _ = x_sample.shape

    c_all = jnp.concatenate([c_prompt, c_sample], axis=0)
    m_all = c_all.shape[0]
    c_all = _pad_to(c_all, 0, -(-m_all // SUBLANES) * SUBLANES)
    mods = ada_project(c_all, ada_w, ada_b, 1536)
    kvmod = ada_project(c_all, kv_ada_w[None], kv_ada_b[None], 1024)[0]
    cmp_consts = _compress_consts(cmp_pos, cmp_w1, cmp_b1, cmp_w2)

    def make_prompt_attend(x, kv_sh, kv_sc):
        nch = t_p // CMP_STRIDE
        kv, kvt = kv_project_with_transpose(x, kv_norm_g, kv_sh, kv_sc, w_kv, 4 * g_kv * hd, bp, t_p)
        slabs = 2 * g_kv * hd // LANES
        kvc = kv.reshape(bp, t_p, 6 * g_kv * hd)[:, :, :slabs * LANES].reshape(bp, t_p * slabs, LANES)
        cmp_kv = compress_call([kvc], [pl.BlockSpec((1, t_p * slabs, LANES), lambda b: (b, 0, 0))],
                               (bp,), None, bp, nch, t_p, slabs, cmp_consts)
        return kv, lambda qg: nsa_prompt_attend(qg, cmp_kv, kvt, bp, t_p)

    y_p, kv_p, shift_p, wkv_p = _run_trunk(
        x_prompt.reshape(bp * t_p, d), mods[:, :bp], kvmod[:bp],
        jnp.zeros((N_A, bp, d), F32), jnp.zeros((N_A, bp, RWKV_HEADS, RWKV_HEAD_DIM, RWKV_HEAD_DIM), F32),
        P, bp, t_p, make_prompt_attend, bp, min(64, t_p))

    n_pages = page_table.shape[1]
    page_rows = cache_nsa_kv.shape[1]
    cache_pages = jnp.transpose(cache_nsa_kv, (0, 2, 3, 4, 1)).reshape(cache_nsa_kv.shape[0], 4 * g_kv * hd, page_rows)
    wb = cache_win_kv.shape[1]
    cache_win = jnp.transpose(cache_win_kv, (0, 2, 3, 4, 1)).reshape(bs, 2 * g_kv * hd, wb)

    def make_sample_attend(x, kv_sh, kv_sc):
        nch = (n_pages * page_rows + t_s) // CMP_STRIDE
        assert nch * CMP_STRIDE == n_pages * page_rows
        kv = norm_mod_matmul(x, kv_norm_g, kv_sh, kv_sc, w_kv, t_s)
        specs = [pl.BlockSpec((1, 2 * g_kv * hd, page_rows), lambda b, pt, j=j: (pt[b, j], 0, 0))
                 for j in range(n_pages)]
        cmp_kv = compress_call([cache_pages] * n_pages, specs, (bs,), page_table, bs, nch, page_rows, 0,
                               cmp_consts)
        return kv, lambda qg: nsa_sample_attend(qg, cmp_kv, cache_pages, page_table, cache_win, kv, bs, t_s)

    y_s, kv_s, shift_s, wkv_s = _run_trunk(
        x_sample.reshape(bs * t_s, d), mods[:, bp:bp + bs], kvmod[bp:bp + bs],
        state_shift, state_wkv, P, bs, t_s, make_sample_attend, 4, t_s)

    kv_p5 = kv_p.reshape(bp, t_p, 6, g_kv, hd)
    kv_s5 = kv_s.reshape(bs, t_s, 6, g_kv, hd)
    win_p = kv_p5[:, -min(WINDOW, t_p):, 4:]
    win_all = jnp.concatenate([cache_win_kv, kv_s5[:, :, 4:]], axis=1)
    win_s = win_all[:, -min(WINDOW, win_all.shape[1]):]
    return (y_p.reshape(bp, t_p, d), y_s.reshape(bs, t_s, d), kv_p5[:, :, :4], kv_s5[:, :, :4],
            win_p, win_s, wkv_p, wkv_s, shift_p, shift_s)
```
